```python
import math
import jax, jax.numpy as jnp
from jax import lax
import numpy as np

D_MODEL = 1024
BATCH = 16
SEQ = 4096
DEPTH = 4

MEM_LEN = 256
CHUNK = 128
Q_BLOCK = 128
A_GROUPS = 4
A_WIDTH = D_MODEL // 2
A_GROUP_DIM = A_WIDTH // A_GROUPS
B_GROUPS = 4
B_WIDTH = D_MODEL // 2
B_GROUP_DIM = B_WIDTH // B_GROUPS
CONV_WIDTH = 31
C_HEADS = 16
C_HEAD_DIM = D_MODEL // C_HEADS
MEM_HEADS = 4
MEM_HEAD_DIM = D_MODEL // MEM_HEADS
D_FF = ((8 * D_MODEL + 3 * 256 - 1) // (3 * 256)) * 256
DEEPNORM_ALPHA = (2.0 * DEPTH) ** 0.25
DEEPNORM_BETA = (8.0 * DEPTH) ** -0.25
LN_EPS = 1e-5
N_EVEN = (DEPTH + 1) // 2
N_ODD = DEPTH // 2

kernel_name = "hybrid_gmlp_conformer_stickbreaking_deepnorm"


def _layer_norm(x, g, b):
    xf = x.astype(jnp.float32)
    mu = jnp.mean(xf, axis=-1, keepdims=True)
    var = jnp.mean(jnp.square(xf - mu), axis=-1, keepdims=True)
    y = (xf - mu) * lax.rsqrt(var + LN_EPS)
    return (y * g.astype(jnp.float32) + b.astype(jnp.float32)).astype(x.dtype)


def _post_norm(x, f, g, b):
    return _layer_norm(DEEPNORM_ALPHA * x + f, g, b)


def _gmlp_chunked(u, v, ln_g, ln_b, w_s, b_s):
    bn, s, _ = u.shape
    u = jax.nn.gelu(u, approximate=False)
    v = jax.nn.gelu(v, approximate=False)
    vg = v.reshape(bn, s // CHUNK, CHUNK, A_GROUPS, A_GROUP_DIM)
    vg = _layer_norm(vg, ln_g.reshape(A_GROUPS, A_GROUP_DIM), ln_b.reshape(A_GROUPS, A_GROUP_DIM))
    causal = jnp.tril(jnp.ones((CHUNK, CHUNK), dtype=bool))
    w = jnp.where(causal[None], w_s, jnp.zeros_like(w_s))
    mixed = jnp.einsum('gts,bcsgd->bctgd', w, vg) + b_s.T[:, :, None]
    return u * mixed.reshape(bn, s, A_WIDTH)


def _conformer_conv(a, gate, conv_w, conv_b, gn_g, gn_b):
    h = a * jax.nn.sigmoid(gate)
    h = lax.conv_general_dilated(
        h, conv_w[:, None, :].astype(h.dtype), window_strides=(1,),
        padding=[(CONV_WIDTH - 1, 0)],
        dimension_numbers=('NWC', 'WIO', 'NWC'),
        feature_group_count=B_WIDTH) + conv_b
    bn, s, _ = h.shape
    hg = _layer_norm(h.reshape(bn, s, B_GROUPS, B_GROUP_DIM),
                     gn_g.reshape(B_GROUPS, B_GROUP_DIM), gn_b.reshape(B_GROUPS, B_GROUP_DIM))
    return jax.nn.silu(hg).reshape(bn, s, B_WIDTH)


def _stick_breaking(q, k, v):
    s_len = q.shape[2]
    scale = C_HEAD_DIM ** -0.5
    outs = []
    for i in range(s_len // Q_BLOCK):
        t0 = i * Q_BLOCK
        kend = t0 + Q_BLOCK
        qb = q[:, :, t0:kend]
        kb = k[:, :, :kend]
        vb = v[:, :, :kend]
        z = jnp.einsum('bhqd,bhkd->bhqk', qb, kb,
                       preferred_element_type=jnp.float32) * scale
        t_pos = t0 + jnp.arange(Q_BLOCK)[:, None]
        s_pos = jnp.arange(kend)[None, :]
        strict = s_pos < t_pos
        log_not = jnp.where(strict, jax.nn.log_sigmoid(-z), 0.0)
        later = lax.cumsum(log_not, axis=log_not.ndim - 1, reverse=True) - log_not
        att = jnp.where(strict, jnp.exp(jax.nn.log_sigmoid(z) + later), 0.0)
        outs.append(jnp.einsum('bhqk,bhkd->bhqd', att.astype(vb.dtype), vb))
    return jnp.concatenate(outs, axis=2)


def _memory_cross_attention(x, mem, wq, wk, wv, wo):
    bn, s, _ = x.shape
    m = mem.shape[1]
    q = (x @ wq).reshape(bn, s, MEM_HEADS, MEM_HEAD_DIM)
    k = (mem @ wk).reshape(bn, m, MEM_HEADS, MEM_HEAD_DIM)
    v = (mem @ wv).reshape(bn, m, MEM_HEADS, MEM_HEAD_DIM)
    sc = jnp.einsum('bqhd,bkhd->bhqk', q, k,
                    preferred_element_type=jnp.float32) * (MEM_HEAD_DIM ** -0.5)
    p = jax.nn.softmax(sc, axis=-1).astype(v.dtype)
    o = jnp.einsum('bhqk,bkhd->bqhd', p, v).reshape(bn, s, D_MODEL)
    return o @ wo


def _swiglu(x, w1, w3, w2):
    return (jax.nn.silu(x @ w1) * (x @ w3)) @ w2


def setup_inputs(seed: int = 0) -> dict:
    key = jax.random.key(seed)
    ks = jax.random.split(key, 24)
    f32 = jnp.float32
    nrm = lambda k, shape, sc: jax.random.normal(k, shape, f32) * sc
    d = D_MODEL
    return {
        "x": nrm(ks[0], (BATCH, SEQ, d), 1.0),
        "mem": nrm(ks[1], (BATCH, MEM_LEN, d), 1.0),
        "w_in_ab": nrm(ks[2], (N_EVEN, d, 2 * A_WIDTH + 2 * B_WIDTH), d ** -0.5),
        "gmlp_ln_g": 1.0 + nrm(ks[3], (N_EVEN, A_WIDTH), 0.02),
        "gmlp_ln_b": nrm(ks[4], (N_EVEN, A_WIDTH), 0.02),
        "gmlp_w_s": nrm(ks[5], (N_EVEN, A_GROUPS, CHUNK, CHUNK), CHUNK ** -0.5),
        "gmlp_b_s": 1.0 + nrm(ks[6], (N_EVEN, A_GROUPS, CHUNK), 0.02),
        "conv_w": nrm(ks[7], (N_EVEN, CONV_WIDTH, B_WIDTH), CONV_WIDTH ** -0.5),
        "conv_b": nrm(ks[8], (N_EVEN, B_WIDTH), 0.02),
        "conv_gn_g": 1.0 + nrm(ks[9], (N_EVEN, B_WIDTH), 0.02),
        "conv_gn_b": nrm(ks[10], (N_EVEN, B_WIDTH), 0.02),
        "w_out_ab": nrm(ks[11], (N_EVEN, A_WIDTH + B_WIDTH, d), (A_WIDTH + B_WIDTH) ** -0.5 * DEEPNORM_BETA),
        "w_qkv_c": nrm(ks[12], (N_ODD, d, 3 * d), d ** -0.5),
        "w_out_c": nrm(ks[13], (N_ODD, d, d), d ** -0.5 * DEEPNORM_BETA),
        "mem_wq": nrm(ks[14], (DEPTH, d, d), d ** -0.5),
        "mem_wk": nrm(ks[15], (DEPTH, d, d), d ** -0.5),
        "mem_wv": nrm(ks[16], (DEPTH, d, d), d ** -0.5),
        "mem_wo": nrm(ks[17], (DEPTH, d, d), d ** -0.5 * DEEPNORM_BETA),
        "ffn_w1": nrm(ks[18], (DEPTH, d, D_FF), d ** -0.5),
        "ffn_w3": nrm(ks[19], (DEPTH, d, D_FF), d ** -0.5),
        "ffn_w2": nrm(ks[20], (DEPTH, D_FF, d), D_FF ** -0.5 * DEEPNORM_BETA),
        "ln_g": 1.0 + nrm(ks[21], (DEPTH, 3, d), 0.02),
        "ln_b": nrm(ks[22], (DEPTH, 3, d), 0.02),
    }


def reference(x, mem, w_in_ab, gmlp_ln_g, gmlp_ln_b, gmlp_w_s, gmlp_b_s, conv_w, conv_b,
              conv_gn_g, conv_gn_b, w_out_ab, w_qkv_c, w_out_c, mem_wq, mem_wk, mem_wv,
              mem_wo, ffn_w1, ffn_w3, ffn_w2, ln_g, ln_b):
    bn, s, d = x.shape
    for layer in range(DEPTH):
        if layer % 2 == 0:
            e = layer // 2
            h = x @ w_in_ab[e]
            u = h[..., :A_WIDTH]
            v = h[..., A_WIDTH:2 * A_WIDTH]
            a = h[..., 2 * A_WIDTH:2 * A_WIDTH + B_WIDTH]
            gt = h[..., 2 * A_WIDTH + B_WIDTH:]
            ya = _gmlp_chunked(u, v, gmlp_ln_g[e], gmlp_ln_b[e], gmlp_w_s[e], gmlp_b_s[e])
            yb = _conformer_conv(a, gt, conv_w[e], conv_b[e], conv_gn_g[e], conv_gn_b[e])
            mix = jnp.concatenate([ya, yb], axis=-1) @ w_out_ab[e]
        else:
            o = layer // 2
            qkv = (x @ w_qkv_c[o]).reshape(bn, s, 3, C_HEADS, C_HEAD_DIM)
            q = qkv[:, :, 0].transpose(0, 2, 1, 3)
            k = qkv[:, :, 1].transpose(0, 2, 1, 3)
            v = qkv[:, :, 2].transpose(0, 2, 1, 3)
            y = _stick_breaking(q, k, v).transpose(0, 2, 1, 3).reshape(bn, s, d)
            mix = y @ w_out_c[o]
        x = _post_norm(x, mix, ln_g[layer, 0], ln_b[layer, 0])
        cross = _memory_cross_attention(x, mem, mem_wq[layer], mem_wk[layer], mem_wv[layer], mem_wo[layer])
        x = _post_norm(x, cross, ln_g[layer, 1], ln_b[layer, 1])
        x = _post_norm(x, _swiglu(x, ffn_w1[layer], ffn_w3[layer], ffn_w2[layer]), ln_g[layer, 2], ln_b[layer, 2])
    return x
```

```python
import functools

import jax
import jax.numpy as jnp
from jax import lax
from jax.experimental import pallas as pl
from jax.experimental.pallas import tpu as pltpu

F32 = jnp.float32
BF16 = jnp.bfloat16

DEPTH = 4
CHUNK = 128
A_GROUPS = 4
B_GROUPS = 4
GROUP_DIM = 128
CONV_WIDTH = 31
C_HEADS = 16
C_HEAD_DIM = 64
MEM_HEADS = 4
DEEPNORM_ALPHA = (2.0 * DEPTH) ** 0.25
LN_EPS = 1e-5

LANES = 128
CONV_HALO = 32
Q_BLOCK = 128
K_TILE = 256
STICK_EXIT = 88.0

TM_FFN = 512
TM_CROSS = 512
TM_EVEN = 256
TM_PROJ = 512
VMEM_LIMIT = 56 * 1024 * 1024


def _resident(shape):
    nd = len(shape)
    return pl.BlockSpec(shape, lambda *_: (0,) * nd, pipeline_mode=pl.Buffered(1))


def _params(n_axes):
    return pltpu.CompilerParams(dimension_semantics=("arbitrary",) * n_axes,
                                vmem_limit_bytes=VMEM_LIMIT)


def _ln_rows(y, g, b):
    mu = jnp.mean(y, axis=-1, keepdims=True)
    yc = y - mu
    var = jnp.mean(yc * yc, axis=-1, keepdims=True)
    return yc * lax.rsqrt(var + LN_EPS) * g + b


def _gelu_exact(x):
    return 0.5 * x * (1.0 + lax.erf(x * (0.5 ** 0.5)))


def _silu(x):
    return x * jax.nn.sigmoid(x)


def _mm(a, b):
    return jnp.dot(a, b, preferred_element_type=F32)


def _ffn_kernel(x_ref, w1_ref, w3_ref, w2_ref, g_ref, b_ref, o_ref):
    x = x_ref[...]
    xb = x.astype(BF16)
    h1 = _mm(xb, w1_ref[...])
    h3 = _mm(xb, w3_ref[...])
    act = (_silu(h1) * h3).astype(BF16)
    y = _mm(act, w2_ref[...])
    o_ref[...] = _ln_rows(DEEPNORM_ALPHA * x + y, g_ref[...], b_ref[...])


def _ffn(x2, w1, w3, w2, g, b):
    n, d = x2.shape
    dff = w1.shape[1]
    tile = pl.BlockSpec((TM_FFN, d), lambda i: (i, 0))
    return pl.pallas_call(
        _ffn_kernel,
        grid=(n // TM_FFN,),
        in_specs=[tile, _resident((d, dff)), _resident((d, dff)), _resident((dff, d)),
                  _resident((1, d)), _resident((1, d))],
        out_specs=tile,
        out_shape=jax.ShapeDtypeStruct((n, d), F32),
        compiler_params=_params(1),
        name="ffn_ln",
    )(x2, w1, w3, w2, g, b)


def _memkv_kernel(mem_ref, wkt_ref, wv_ref, kt_ref, v_ref):
    mb = mem_ref[0].astype(BF16)
    kt = lax.dot_general(wkt_ref[...], mb, (((1,), (1,)), ((), ())), preferred_element_type=F32)
    kt_ref[0] = kt.astype(BF16)
    v_ref[0] = _mm(mb, wv_ref[...]).astype(BF16)


def _memkv(mem, wkt, wv):
    bn, m, d = mem.shape
    return pl.pallas_call(
        _memkv_kernel,
        grid=(bn,),
        in_specs=[pl.BlockSpec((1, m, d), lambda i: (i, 0, 0)), _resident((d, d)), _resident((d, d))],
        out_specs=[pl.BlockSpec((1, d, m), lambda i: (i, 0, 0)), pl.BlockSpec((1, m, d), lambda i: (i, 0, 0))],
        out_shape=[jax.ShapeDtypeStruct((bn, d, m), BF16), jax.ShapeDtypeStruct((bn, m, d), BF16)],
        compiler_params=_params(1),
        name="mem_kv",
    )(mem, wkt, wv)


def _cross_kernel(x_ref, kt_ref, v_ref, wq_ref, wo_ref, g_ref, b_ref, o_ref):
    x = x_ref[0]
    d = x.shape[-1]
    hd = d // MEM_HEADS
    q = (_mm(x.astype(BF16), wq_ref[...]) * (hd ** -0.5)).astype(BF16)
    outs = []
    for h in range(MEM_HEADS):
        sl = slice(h * hd, (h + 1) * hd)
        s = _mm(q[:, sl], kt_ref[0, sl, :])
        e = jnp.exp(s - jnp.max(s, axis=-1, keepdims=True))
        p = (e / jnp.sum(e, axis=-1, keepdims=True)).astype(BF16)
        outs.append(_mm(p, v_ref[0, :, sl]))
    o = jnp.concatenate(outs, axis=-1).astype(BF16)
    cross = _mm(o, wo_ref[...])
    o_ref[0] = _ln_rows(DEEPNORM_ALPHA * x + cross, g_ref[...], b_ref[...])


def _cross(x, kt, v, wq, wo, g, b):
    bn, s, d = x.shape
    m = v.shape[1]
    tile = pl.BlockSpec((1, TM_CROSS, d), lambda i, j: (i, j, 0))
    return pl.pallas_call(
        _cross_kernel,
        grid=(bn, s // TM_CROSS),
        in_specs=[tile,
                  pl.BlockSpec((1, d, m), lambda i, j: (i, 0, 0)),
                  pl.BlockSpec((1, m, d), lambda i, j: (i, 0, 0)),
                  _resident((d, d)), _resident((d, d)), _resident((1, d)), _resident((1, d))],
        out_specs=tile,
        out_shape=jax.ShapeDtypeStruct((bn, s, d), F32),
        compiler_params=_params(2),
        name="cross_ln",
    )(x, kt, v, wq, wo, g, b)


def _even_kernel(x_ref, win_ref, lng_ref, lnb_ref, ws_ref, bs_ref, cw_ref, cb_ref, gng_ref, gnb_ref,
                 wout_ref, g_ref, b_ref, o_ref, hbuf_ref):
    tm = x_ref.shape[1]
    aw = A_GROUPS * GROUP_DIM
    bw = B_GROUPS * GROUP_DIM
    x = x_ref[0]
    h = _mm(x.astype(BF16), win_ref[...])
    u = _gelu_exact(h[:, 0:aw])
    v = _gelu_exact(h[:, aw:2 * aw])
    a = h[:, 2 * aw:2 * aw + bw]
    gate = h[:, 2 * aw + bw:]

    row = lax.broadcasted_iota(jnp.int32, (CHUNK, CHUNK), 0)
    col = lax.broadcasted_iota(jnp.int32, (CHUNK, CHUNK), 1)
    causal = row >= col
    y_parts = []
    for g in range(A_GROUPS):
        gs = slice(g * GROUP_DIM, (g + 1) * GROUP_DIM)
        vn = _ln_rows(v[:, gs], lng_ref[:, gs], lnb_ref[:, gs]).astype(BF16)
        wg = jnp.where(causal, ws_ref[g], 0.0).astype(BF16)
        mixed = [_mm(wg, vn[c * CHUNK:(c + 1) * CHUNK, :]) + bs_ref[:, gs]
                 for c in range(tm // CHUNK)]
        y_parts.append(u[:, gs] * jnp.concatenate(mixed, axis=0))

    @pl.when(pl.program_id(1) == 0)
    def _():
        hbuf_ref[0:CONV_HALO, :] = jnp.zeros((CONV_HALO, bw), F32)

    hbuf_ref[CONV_HALO:CONV_HALO + tm, :] = a * jax.nn.sigmoid(gate)
    strip = 32
    conv_rows = []
    for r0 in range(0, tm, strip):
        acc = jnp.broadcast_to(cb_ref[...], (strip, bw))
        for j in range(CONV_WIDTH):
            off = r0 + CONV_HALO - (CONV_WIDTH - 1) + j
            acc = acc + cw_ref[j:j + 1, :] * hbuf_ref[off:off + strip, :]
        conv_rows.append(acc)
    conv = jnp.concatenate(conv_rows, axis=0)
    hbuf_ref[0:CONV_HALO, :] = hbuf_ref[tm:tm + CONV_HALO, :]
    for g in range(B_GROUPS):
        gs = slice(g * GROUP_DIM, (g + 1) * GROUP_DIM)
        y_parts.append(_silu(_ln_rows(conv[:, gs], gng_ref[:, gs], gnb_ref[:, gs])))

    ycat = jnp.concatenate(y_parts, axis=-1).astype(BF16)
    mix = _mm(ycat, wout_ref[...])
    o_ref[0] = _ln_rows(DEEPNORM_ALPHA * x + mix, g_ref[...], b_ref[...])


def _even_mixer(x, win, lng, lnb, ws, bs_full, cw, cb, gng, gnb, wout, g, b):
    bn, s, d = x.shape
    aw = A_GROUPS * GROUP_DIM
    bw = B_GROUPS * GROUP_DIM
    tile = pl.BlockSpec((1, TM_EVEN, d), lambda i, j: (i, j, 0))
    return pl.pallas_call(
        _even_kernel,
        grid=(bn, s // TM_EVEN),
        in_specs=[tile, _resident(win.shape), _resident((1, aw)), _resident((1, aw)),
                  _resident(ws.shape), _resident(bs_full.shape), _resident(cw.shape), _resident((1, bw)),
                  _resident((1, bw)), _resident((1, bw)), _resident(wout.shape),
                  _resident((1, d)), _resident((1, d))],
        out_specs=tile,
        out_shape=jax.ShapeDtypeStruct((bn, s, d), F32),
        scratch_shapes=[pltpu.VMEM((TM_EVEN + CONV_HALO, bw), F32)],
        compiler_params=_params(2),
        name="even_mixer_ln",
    )(x, win, lng, lnb, ws, bs_full, cw, cb, gng, gnb, wout, g, b)


def _qkv_kernel(x_ref, w_ref, q_ref, k_ref, v_ref):
    d = x_ref.shape[-1]
    qkv = _mm(x_ref[...].astype(BF16), w_ref[...])
    q_ref[...] = (qkv[:, 0:d] * (C_HEAD_DIM ** -0.5)).astype(BF16)
    k_ref[...] = qkv[:, d:2 * d].astype(BF16)
    v_ref[...] = qkv[:, 2 * d:3 * d].astype(BF16)


def _qkv(x2, w):
    n, d = x2.shape
    tile = pl.BlockSpec((TM_PROJ, d), lambda i: (i, 0))
    out = jax.ShapeDtypeStruct((n, d), BF16)
    return pl.pallas_call(
        _qkv_kernel,
        grid=(n // TM_PROJ,),
        in_specs=[tile, _resident(w.shape)],
        out_specs=[tile, tile, tile],
        out_shape=[out, out, out],
        compiler_params=_params(1),
        name="qkv_proj",
    )(x2, w)


def _stick_kernel(q_ref, k_ref, v_ref, o_ref):
    seq = q_ref.shape[1]
    rj = lax.broadcasted_iota(jnp.int32, (K_TILE, K_TILE), 0)
    cs = lax.broadcasted_iota(jnp.int32, (K_TILE, K_TILE), 1)
    later_in_tile = (rj > cs).astype(BF16)
    lane = lax.broadcasted_iota(jnp.int32, (Q_BLOCK, LANES), 1)
    key_off = lax.broadcasted_iota(jnp.int32, (Q_BLOCK, K_TILE), 1)
    row_off = lax.broadcasted_iota(jnp.int32, (Q_BLOCK, 1), 0)

    def tile(qm, w0, limit, carry, acc):
        kt = k_ref[0, pl.ds(w0, K_TILE), :]
        vt = v_ref[0, pl.ds(w0, K_TILE), :]
        z = lax.dot_general(qm, kt, (((1,), (1,)), ((), ())), preferred_element_type=F32)
        valid = (w0 + key_off) < limit
        sp = jnp.maximum(z, 0.0) + jnp.log(1.0 + jnp.exp(-jnp.abs(z)))
        sp = jnp.where(valid, sp, 0.0)
        hi = sp.astype(BF16)
        lo = (sp - hi.astype(F32)).astype(BF16)
        cs2 = _mm(jnp.concatenate([hi, lo], axis=0), later_in_tile)
        csum = cs2[:Q_BLOCK] + cs2[Q_BLOCK:]
        att = jnp.where(valid, jnp.exp((z - sp) - csum - carry), 0.0)
        acc = acc + _mm(att.astype(BF16), vt)
        carry = carry + jnp.sum(sp, axis=1, keepdims=True)
        return carry, acc

    def q_block(qi, _):
        t0 = pl.multiple_of(qi * Q_BLOCK, Q_BLOCK)
        q2 = q_ref[0, pl.ds(t0, Q_BLOCK), :]
        t_pos = t0 + row_off
        w_diag = pl.multiple_of(jnp.maximum(t0 + Q_BLOCK - K_TILE, 0), Q_BLOCK)
        res = []
        for hh in range(LANES // C_HEAD_DIM):
            in_head = (lane >= hh * C_HEAD_DIM) & (lane < (hh + 1) * C_HEAD_DIM)
            qm = jnp.where(in_head, q2, jnp.zeros_like(q2))
            carry, acc = tile(qm, w_diag, t_pos,
                              jnp.zeros((Q_BLOCK, 1), F32), jnp.zeros((Q_BLOCK, LANES), F32))

            def cond(st):
                ks, carry, _ = st
                return jnp.logical_and(ks > 0, jnp.min(carry) < STICK_EXIT)

            def body(st):
                ks, carry, acc = st
                w0 = pl.multiple_of(jnp.maximum(ks - K_TILE, 0), Q_BLOCK)
                carry, acc = tile(qm, w0, ks, carry, acc)
                return w0, carry, acc

            _, _, acc = lax.while_loop(cond, body, (w_diag, carry, acc))
            res.append((in_head, acc))
        out = jnp.where(res[0][0], res[0][1], res[1][1])
        o_ref[0, pl.ds(t0, Q_BLOCK), :] = out.astype(o_ref.dtype)
        return 0

    lax.fori_loop(0, seq // Q_BLOCK, q_block, 0)


def _stick(q, k, v):
    bn, s, d = q.shape
    blk = pl.BlockSpec((1, s, LANES), lambda i, j: (i, 0, j))
    return pl.pallas_call(
        _stick_kernel,
        grid=(bn, d // LANES),
        in_specs=[blk, blk, blk],
        out_specs=blk,
        out_shape=jax.ShapeDtypeStruct((bn, s, d), BF16),
        compiler_params=_params(2),
        name="stick_breaking",
    )(q, k, v)


def _proj_ln_kernel(y_ref, x_ref, w_ref, g_ref, b_ref, o_ref):
    mix = _mm(y_ref[...], w_ref[...])
    o_ref[...] = _ln_rows(DEEPNORM_ALPHA * x_ref[...] + mix, g_ref[...], b_ref[...])


def _proj_ln(y2, x2, w, g, b):
    n, d = x2.shape
    tile = pl.BlockSpec((TM_PROJ, d), lambda i: (i, 0))
    return pl.pallas_call(
        _proj_ln_kernel,
        grid=(n // TM_PROJ,),
        in_specs=[tile, tile, _resident(w.shape), _resident((1, d)), _resident((1, d))],
        out_specs=tile,
        out_shape=jax.ShapeDtypeStruct((n, d), F32),
        compiler_params=_params(1),
        name="out_proj_ln",
    )(y2, x2, w, g, b)


def kernel(x, mem, w_in_ab, gmlp_ln_g, gmlp_ln_b, gmlp_w_s, gmlp_b_s, conv_w, conv_b, conv_gn_g, conv_gn_b, w_out_ab, w_qkv_c, w_out_c, mem_wq, mem_wk, mem_wv, mem_wo, ffn_w1, ffn_w3, ffn_w2, ln_g, ln_b):
    bn, s, d = x.shape
    n = bn * s
    row = lambda a: a.reshape(1, -1).astype(F32)
    bf = lambda a: a.astype(BF16)
    for layer in range(DEPTH):
        if layer % 2 == 0:
            e = layer // 2
            bs_full = jnp.repeat(gmlp_b_s[e].T, GROUP_DIM, axis=1)
            x = _even_mixer(x, bf(w_in_ab[e]), row(gmlp_ln_g[e]), row(gmlp_ln_b[e]), gmlp_w_s[e], bs_full,
                            conv_w[e], row(conv_b[e]), row(conv_gn_g[e]), row(conv_gn_b[e]), bf(w_out_ab[e]),
                            row(ln_g[layer, 0]), row(ln_b[layer, 0]))
        else:
            o = layer // 2
            x2 = x.reshape(n, d)
            q, k, v = _qkv(x2, bf(w_qkv_c[o]))
            y = _stick(q.reshape(bn, s, d), k.reshape(bn, s, d), v.reshape(bn, s, d))
            x = _proj_ln(y.reshape(n, d), x2, bf(w_out_c[o]), row(ln_g[layer, 0]), row(ln_b[layer, 0])).reshape(bn, s, d)
        kt, vm = _memkv(mem, bf(mem_wk[layer].T), bf(mem_wv[layer]))
        x = _cross(x, kt, vm, bf(mem_wq[layer]), bf(mem_wo[layer]), row(ln_g[layer, 1]), row(ln_b[layer, 1]))
        x = _ffn(x.reshape(n, d), bf(ffn_w1[layer]), bf(ffn_w3[layer]), bf(ffn_w2[layer]),
                 row(ln_g[layer, 2]), row(ln_b[layer, 2])).reshape(bn, s, d)
    return x
```

```python
import functools

import jax
import jax.numpy as jnp
from jax import lax
from jax.experimental import pallas as pl
from jax.experimental.pallas import tpu as pltpu

F32 = jnp.float32
BF16 = jnp.bfloat16

DEPTH = 4
CHUNK = 128
A_GROUPS = 4
B_GROUPS = 4
GROUP_DIM = 128
CONV_WIDTH = 31
C_HEADS = 16
C_HEAD_DIM = 64
MEM_HEADS = 4
DEEPNORM_ALPHA = (2.0 * DEPTH) ** 0.25
LN_EPS = 1e-5

LANES = 128
SUBLANES = 8
CONV_HALO = 32
Q_BLOCK = 128
K_TILE = 256
STICK_EXIT = 88.0
STICK_MASKED = -1e30
Q_INFLIGHT = 2

TM_FFN = 512
TM_CROSS = 512
TM_EVEN = 256
TM_PROJ = 512
VMEM_LIMIT = 56 * 1024 * 1024


def _resident(shape):
    nd = len(shape)
    return pl.BlockSpec(shape, lambda *_: (0,) * nd, pipeline_mode=pl.Buffered(1))


def _params(n_axes):
    return pltpu.CompilerParams(dimension_semantics=("arbitrary",) * n_axes,
                                vmem_limit_bytes=VMEM_LIMIT)


def _ln_rows(y, g, b):
    mu = jnp.mean(y, axis=-1, keepdims=True)
    yc = y - mu
    var = jnp.mean(yc * yc, axis=-1, keepdims=True)
    return yc * lax.rsqrt(var + LN_EPS) * g + b


def _gelu_exact(x):
    return 0.5 * x * (1.0 + lax.erf(x * (0.5 ** 0.5)))


def _silu(x):
    return x * jax.nn.sigmoid(x)


def _mm(a, b):
    return jnp.dot(a, b, preferred_element_type=F32)


def _ffn_kernel(x_ref, w1_ref, w3_ref, w2_ref, g_ref, b_ref, o_ref):
    x = x_ref[...]
    xb = x.astype(BF16)
    h1 = _mm(xb, w1_ref[...])
    h3 = _mm(xb, w3_ref[...])
    act = (_silu(h1) * h3).astype(BF16)
    y = _mm(act, w2_ref[...])
    o_ref[...] = _ln_rows(DEEPNORM_ALPHA * x + y, g_ref[...], b_ref[...])


def _ffn(x2, w1, w3, w2, g, b):
    n, d = x2.shape
    dff = w1.shape[1]
    tile = pl.BlockSpec((TM_FFN, d), lambda i: (i, 0))
    return pl.pallas_call(
        _ffn_kernel,
        grid=(n // TM_FFN,),
        in_specs=[tile, _resident((d, dff)), _resident((d, dff)), _resident((dff, d)),
                  _resident((1, d)), _resident((1, d))],
        out_specs=tile,
        out_shape=jax.ShapeDtypeStruct((n, d), F32),
        compiler_params=_params(1),
        name="ffn_ln",
    )(x2, w1, w3, w2, g, b)


def _memkv_kernel(mem_ref, wkt_ref, wv_ref, kt_ref, v_ref):
    mb = mem_ref[0].astype(BF16)
    kt = lax.dot_general(wkt_ref[...], mb, (((1,), (1,)), ((), ())), preferred_element_type=F32)
    kt_ref[0] = kt.astype(BF16)
    v_ref[0] = _mm(mb, wv_ref[...]).astype(BF16)


def _memkv(mem, wkt, wv):
    bn, m, d = mem.shape
    return pl.pallas_call(
        _memkv_kernel,
        grid=(bn,),
        in_specs=[pl.BlockSpec((1, m, d), lambda i: (i, 0, 0)), _resident((d, d)), _resident((d, d))],
        out_specs=[pl.BlockSpec((1, d, m), lambda i: (i, 0, 0)), pl.BlockSpec((1, m, d), lambda i: (i, 0, 0))],
        out_shape=[jax.ShapeDtypeStruct((bn, d, m), BF16), jax.ShapeDtypeStruct((bn, m, d), BF16)],
        compiler_params=_params(1),
        name="mem_kv",
    )(mem, wkt, wv)


def _cross_kernel(x_ref, kt_ref, v_ref, wq_ref, wo_ref, g_ref, b_ref, o_ref):
    x = x_ref[0]
    d = x.shape[-1]
    hd = d // MEM_HEADS
    q = (_mm(x.astype(BF16), wq_ref[...]) * (hd ** -0.5)).astype(BF16)
    outs = []
    for h in range(MEM_HEADS):
        sl = slice(h * hd, (h + 1) * hd)
        s = _mm(q[:, sl], kt_ref[0, sl, :])
        e = jnp.exp(s - jnp.max(s, axis=-1, keepdims=True))
        p = (e / jnp.sum(e, axis=-1, keepdims=True)).astype(BF16)
        outs.append(_mm(p, v_ref[0, :, sl]))
    o = jnp.concatenate(outs, axis=-1).astype(BF16)
    cross = _mm(o, wo_ref[...])
    o_ref[0] = _ln_rows(DEEPNORM_ALPHA * x + cross, g_ref[...], b_ref[...])


def _cross(x, kt, v, wq, wo, g, b):
    bn, s, d = x.shape
    m = v.shape[1]
    tile = pl.BlockSpec((1, TM_CROSS, d), lambda i, j: (i, j, 0))
    return pl.pallas_call(
        _cross_kernel,
        grid=(bn, s // TM_CROSS),
        in_specs=[tile,
                  pl.BlockSpec((1, d, m), lambda i, j: (i, 0, 0)),
                  pl.BlockSpec((1, m, d), lambda i, j: (i, 0, 0)),
                  _resident((d, d)), _resident((d, d)), _resident((1, d)), _resident((1, d))],
        out_specs=tile,
        out_shape=jax.ShapeDtypeStruct((bn, s, d), F32),
        compiler_params=_params(2),
        name="cross_ln",
    )(x, kt, v, wq, wo, g, b)


def _even_kernel(x_ref, win_ref, lng_ref, lnb_ref, ws_ref, bs_ref, cw_ref, cb_ref, gng_ref, gnb_ref,
                 wout_ref, g_ref, b_ref, o_ref, hbuf_ref, shift_ref):
    tm = x_ref.shape[1]
    aw = A_GROUPS * GROUP_DIM
    bw = B_GROUPS * GROUP_DIM
    x = x_ref[0]
    h = _mm(x.astype(BF16), win_ref[...])
    u = _gelu_exact(h[:, 0:aw])
    v = _gelu_exact(h[:, aw:2 * aw])
    a = h[:, 2 * aw:2 * aw + bw]
    gate = h[:, 2 * aw + bw:]

    row = lax.broadcasted_iota(jnp.int32, (CHUNK, CHUNK), 0)
    col = lax.broadcasted_iota(jnp.int32, (CHUNK, CHUNK), 1)
    causal = row >= col
    y_parts = []
    for g in range(A_GROUPS):
        gs = slice(g * GROUP_DIM, (g + 1) * GROUP_DIM)
        vn = _ln_rows(v[:, gs], lng_ref[:, gs], lnb_ref[:, gs]).astype(BF16)
        wg = jnp.where(causal, ws_ref[g], 0.0).astype(BF16)
        mixed = [_mm(wg, vn[c * CHUNK:(c + 1) * CHUNK, :]) + bs_ref[:, gs]
                 for c in range(tm // CHUNK)]
        y_parts.append(u[:, gs] * jnp.concatenate(mixed, axis=0))

    @pl.when(pl.program_id(1) == 0)
    def _():
        hbuf_ref[0:CONV_HALO, :] = jnp.zeros((CONV_HALO, bw), F32)

    hbuf_ref[CONV_HALO:CONV_HALO + tm, :] = a * jax.nn.sigmoid(gate)
    first = CONV_HALO - (CONV_WIDTH - 1)
    span = tm + CONV_HALO - SUBLANES
    for s in range(1, SUBLANES):
        shift_ref[s - 1, 0:span, :] = hbuf_ref[s:s + span, :]
    strip = 32
    conv_rows = []
    for r0 in range(0, tm, strip):
        acc = jnp.broadcast_to(cb_ref[...], (strip, bw))
        for j in range(CONV_WIDTH):
            s = (first + j) % SUBLANES
            a0 = r0 + (first + j) // SUBLANES * SUBLANES
            src = hbuf_ref[a0:a0 + strip, :] if s == 0 else shift_ref[s - 1, a0:a0 + strip, :]
            acc = acc + cw_ref[j:j + 1, :] * src
        conv_rows.append(acc)
    conv = jnp.concatenate(conv_rows, axis=0)
    hbuf_ref[0:CONV_HALO, :] = hbuf_ref[tm:tm + CONV_HALO, :]
    for g in range(B_GROUPS):
        gs = slice(g * GROUP_DIM, (g + 1) * GROUP_DIM)
        y_parts.append(_silu(_ln_rows(conv[:, gs], gng_ref[:, gs], gnb_ref[:, gs])))

    ycat = jnp.concatenate(y_parts, axis=-1).astype(BF16)
    mix = _mm(ycat, wout_ref[...])
    o_ref[0] = _ln_rows(DEEPNORM_ALPHA * x + mix, g_ref[...], b_ref[...])


def _even_mixer(x, win, lng, lnb, ws, bs_full, cw, cb, gng, gnb, wout, g, b):
    bn, s, d = x.shape
    aw = A_GROUPS * GROUP_DIM
    bw = B_GROUPS * GROUP_DIM
    tile = pl.BlockSpec((1, TM_EVEN, d), lambda i, j: (i, j, 0))
    return pl.pallas_call(
        _even_kernel,
        grid=(bn, s // TM_EVEN),
        in_specs=[tile, _resident(win.shape), _resident((1, aw)), _resident((1, aw)),
                  _resident(ws.shape), _resident(bs_full.shape), _resident(cw.shape), _resident((1, bw)),
                  _resident((1, bw)), _resident((1, bw)), _resident(wout.shape),
                  _resident((1, d)), _resident((1, d))],
        out_specs=tile,
        out_shape=jax.ShapeDtypeStruct((bn, s, d), F32),
        scratch_shapes=[pltpu.VMEM((TM_EVEN + CONV_HALO, bw), F32),
                        pltpu.VMEM((SUBLANES - 1, TM_EVEN + CONV_HALO - SUBLANES, bw), F32)],
        compiler_params=_params(2),
        name="even_mixer_ln",
    )(x, win, lng, lnb, ws, bs_full, cw, cb, gng, gnb, wout, g, b)


def _qkv_kernel(x_ref, w_ref, q_ref, k_ref, v_ref):
    d = x_ref.shape[-1]
    qkv = _mm(x_ref[...].astype(BF16), w_ref[...])
    q_ref[...] = (qkv[:, 0:d] * (C_HEAD_DIM ** -0.5)).astype(BF16)
    k_ref[...] = qkv[:, d:2 * d].astype(BF16)
    v_ref[...] = qkv[:, 2 * d:3 * d].astype(BF16)


def _qkv(x2, w):
    n, d = x2.shape
    tile = pl.BlockSpec((TM_PROJ, d), lambda i: (i, 0))
    out = jax.ShapeDtypeStruct((n, d), BF16)
    return pl.pallas_call(
        _qkv_kernel,
        grid=(n // TM_PROJ,),
        in_specs=[tile, _resident(w.shape)],
        out_specs=[tile, tile, tile],
        out_shape=[out, out, out],
        compiler_params=_params(1),
        name="qkv_proj",
    )(x2, w)


def _stick_kernel(q_ref, k_ref, v_ref, o_ref):
    seq = q_ref.shape[1]
    heads = LANES // C_HEAD_DIM
    rows = heads * Q_BLOCK
    rj = lax.broadcasted_iota(jnp.int32, (K_TILE, K_TILE), 0)
    cs = lax.broadcasted_iota(jnp.int32, (K_TILE, K_TILE), 1)
    later_in_tile = (rj > cs).astype(BF16)
    lane_head = lax.broadcasted_iota(jnp.int32, (rows, LANES), 1) // C_HEAD_DIM
    row_head = lax.broadcasted_iota(jnp.int32, (rows, LANES), 0) // Q_BLOCK
    in_head = lane_head == row_head
    out_lane_head = lax.broadcasted_iota(jnp.int32, (Q_BLOCK, LANES), 1) // C_HEAD_DIM
    key_off = lax.broadcasted_iota(jnp.int32, (rows, K_TILE), 1)
    row_off = lax.broadcasted_iota(jnp.int32, (rows, 1), 0) % Q_BLOCK

    def tile(qs, w0, limit, carry, acc):
        kt = k_ref[0, pl.ds(w0, K_TILE), :]
        vt = v_ref[0, pl.ds(w0, K_TILE), :]
        z = lax.dot_general(qs, kt, (((1,), (1,)), ((), ())), preferred_element_type=F32)
        z = jnp.where(key_off < limit - w0, z, STICK_MASKED)
        sp = jnp.maximum(z, 0.0) + jnp.log(1.0 + jnp.exp(-jnp.abs(z)))
        hi = sp.astype(BF16)
        lo = (sp - hi.astype(F32)).astype(BF16)
        cs2 = _mm(jnp.concatenate([hi, lo], axis=0), later_in_tile)
        csum = cs2[:rows] + cs2[rows:]
        att = jnp.exp((z - sp) - csum - carry)
        acc = acc + _mm(att.astype(BF16), vt)
        carry = carry + jnp.sum(sp, axis=1, keepdims=True)
        return carry, acc

    def q_group(gi, _):
        qs, ks0, carries, accs = [], [], [], []
        for r in range(Q_INFLIGHT):
            t0 = pl.multiple_of((gi * Q_INFLIGHT + r) * Q_BLOCK, Q_BLOCK)
            q2 = q_ref[0, pl.ds(t0, Q_BLOCK), :]
            q2 = jnp.concatenate([q2] * heads, axis=0)
            q_r = jnp.where(in_head, q2, jnp.zeros_like(q2))
            w_diag = pl.multiple_of(jnp.maximum(t0 + Q_BLOCK - K_TILE, 0), Q_BLOCK)
            carry, acc = tile(q_r, w_diag, t0 + row_off,
                              jnp.zeros((rows, 1), F32), jnp.zeros((rows, LANES), F32))
            qs.append(q_r)
            ks0.append(w_diag)
            carries.append(carry)
            accs.append(acc)

        def cond(st):
            ks, carries, _ = st
            open_carry = [jnp.where(k > 0, c, STICK_EXIT) for k, c in zip(ks, carries)]
            return jnp.min(functools.reduce(jnp.minimum, open_carry)) < STICK_EXIT

        def body(st):
            ks, carries, accs = st
            new_ks, new_c, new_a = [], [], []
            for r in range(Q_INFLIGHT):
                w0 = pl.multiple_of(jnp.maximum(ks[r] - K_TILE, 0), Q_BLOCK)
                c, a = tile(qs[r], w0, ks[r], carries[r], accs[r])
                new_ks.append(w0)
                new_c.append(c)
                new_a.append(a)
            return new_ks, new_c, new_a

        _, _, accs = lax.while_loop(cond, body, (ks0, carries, accs))
        for r in range(Q_INFLIGHT):
            t0 = pl.multiple_of((gi * Q_INFLIGHT + r) * Q_BLOCK, Q_BLOCK)
            out = accs[r][0:Q_BLOCK]
            for hh in range(1, heads):
                out = jnp.where(out_lane_head == hh, accs[r][hh * Q_BLOCK:(hh + 1) * Q_BLOCK], out)
            o_ref[0, pl.ds(t0, Q_BLOCK), :] = out.astype(o_ref.dtype)
        return 0

    lax.fori_loop(0, seq // (Q_BLOCK * Q_INFLIGHT), q_group, 0)


def _stick(q, k, v):
    bn, s, d = q.shape
    blk = pl.BlockSpec((1, s, LANES), lambda i, j: (i, 0, j))
    return pl.pallas_call(
        _stick_kernel,
        grid=(bn, d // LANES),
        in_specs=[blk, blk, blk],
        out_specs=blk,
        out_shape=jax.ShapeDtypeStruct((bn, s, d), BF16),
        compiler_params=_params(2),
        name="stick_breaking",
    )(q, k, v)


def _proj_ln_kernel(y_ref, x_ref, w_ref, g_ref, b_ref, o_ref):
    mix = _mm(y_ref[...], w_ref[...])
    o_ref[...] = _ln_rows(DEEPNORM_ALPHA * x_ref[...] + mix, g_ref[...], b_ref[...])


def _proj_ln(y2, x2, w, g, b):
    n, d = x2.shape
    tile = pl.BlockSpec((TM_PROJ, d), lambda i: (i, 0))
    return pl.pallas_call(
        _proj_ln_kernel,
        grid=(n // TM_PROJ,),
        in_specs=[tile, tile, _resident(w.shape), _resident((1, d)), _resident((1, d))],
        out_specs=tile,
        out_shape=jax.ShapeDtypeStruct((n, d), F32),
        compiler_params=_params(1),
        name="out_proj_ln",
    )(y2, x2, w, g, b)


def kernel(x, mem, w_in_ab, gmlp_ln_g, gmlp_ln_b, gmlp_w_s, gmlp_b_s, conv_w, conv_b, conv_gn_g, conv_gn_b, w_out_ab, w_qkv_c, w_out_c, mem_wq, mem_wk, mem_wv, mem_wo, ffn_w1, ffn_w3, ffn_w2, ln_g, ln_b):
    bn, s, d = x.shape
    n = bn * s
    row = lambda a: a.reshape(1, -1).astype(F32)
    bf = lambda a: a.astype(BF16)
    for layer in range(DEPTH):
        if layer % 2 == 0:
            e = layer // 2
            bs_full = jnp.repeat(gmlp_b_s[e].T, GROUP_DIM, axis=1)
            x = _even_mixer(x, bf(w_in_ab[e]), row(gmlp_ln_g[e]), row(gmlp_ln_b[e]), gmlp_w_s[e], bs_full,
                            conv_w[e], row(conv_b[e]), row(conv_gn_g[e]), row(conv_gn_b[e]), bf(w_out_ab[e]),
                            row(ln_g[layer, 0]), row(ln_b[layer, 0]))
        else:
            o = layer // 2
            x2 = x.reshape(n, d)
            q, k, v = _qkv(x2, bf(w_qkv_c[o]))
            y = _stick(q.reshape(bn, s, d), k.reshape(bn, s, d), v.reshape(bn, s, d))
            x = _proj_ln(y.reshape(n, d), x2, bf(w_out_c[o]), row(ln_g[layer, 0]), row(ln_b[layer, 0])).reshape(bn, s, d)
        kt, vm = _memkv(mem, bf(mem_wk[layer].T), bf(mem_wv[layer]))
        x = _cross(x, kt, vm, bf(mem_wq[layer]), bf(mem_wo[layer]), row(ln_g[layer, 1]), row(ln_b[layer, 1]))
        x = _ffn(x.reshape(n, d), bf(ffn_w1[layer]), bf(ffn_w3[layer]), bf(ffn_w2[layer]),
                 row(ln_g[layer, 2]), row(ln_b[layer, 2])).reshape(bn, s, d)
    return x
```

```python
import functools

import jax
import jax.numpy as jnp
from jax import lax
from jax.experimental import pallas as pl
from jax.experimental.pallas import tpu as pltpu

F32 = jnp.float32
BF16 = jnp.bfloat16

DEPTH = 4
CHUNK = 128
A_GROUPS = 4
B_GROUPS = 4
GROUP_DIM = 128
CONV_WIDTH = 31
C_HEADS = 16
C_HEAD_DIM = 64
MEM_HEADS = 4
DEEPNORM_ALPHA = (2.0 * DEPTH) ** 0.25
LN_EPS = 1e-5

LANES = 128
SUBLANES = 8
CONV_HALO = 32
Q_BLOCK = 64
K_TILE = 256
STICK_EXIT = 88.0
STICK_MASKED = -1e30
SUM_GROUP = 4
Q_INFLIGHT = 8

TM_FFN = 512
TM_CROSS = 512
TM_EVEN = 256
TM_PROJ = 512
VMEM_LIMIT = 56 * 1024 * 1024


def _resident(shape):
    nd = len(shape)
    return pl.BlockSpec(shape, lambda *_: (0,) * nd, pipeline_mode=pl.Buffered(1))


def _params(n_axes):
    return pltpu.CompilerParams(dimension_semantics=("arbitrary",) * n_axes,
                                vmem_limit_bytes=VMEM_LIMIT)


def _ln_rows(y, g, b):
    mu = jnp.mean(y, axis=-1, keepdims=True)
    yc = y - mu
    var = jnp.mean(yc * yc, axis=-1, keepdims=True)
    return yc * lax.rsqrt(var + LN_EPS) * g + b


def _gelu_exact(x):
    return 0.5 * x * (1.0 + lax.erf(x * (0.5 ** 0.5)))


def _silu(x):
    return x * jax.nn.sigmoid(x)


def _mm(a, b):
    return jnp.dot(a, b, preferred_element_type=F32)


def _ffn_kernel(x_ref, w1_ref, w3_ref, w2_ref, g_ref, b_ref, o_ref):
    x = x_ref[...]
    xb = x.astype(BF16)
    h1 = _mm(xb, w1_ref[...])
    h3 = _mm(xb, w3_ref[...])
    act = (_silu(h1) * h3).astype(BF16)
    y = _mm(act, w2_ref[...])
    o_ref[...] = _ln_rows(DEEPNORM_ALPHA * x + y, g_ref[...], b_ref[...])


def _ffn(x2, w1, w3, w2, g, b):
    n, d = x2.shape
    dff = w1.shape[1]
    tile = pl.BlockSpec((TM_FFN, d), lambda i: (i, 0))
    return pl.pallas_call(
        _ffn_kernel,
        grid=(n // TM_FFN,),
        in_specs=[tile, _resident((d, dff)), _resident((d, dff)), _resident((dff, d)),
                  _resident((1, d)), _resident((1, d))],
        out_specs=tile,
        out_shape=jax.ShapeDtypeStruct((n, d), F32),
        compiler_params=_params(1),
        name="ffn_ln",
    )(x2, w1, w3, w2, g, b)


def _memkv_kernel(mem_ref, wkt_ref, wv_ref, kt_ref, v_ref):
    mb = mem_ref[0].astype(BF16)
    kt = lax.dot_general(wkt_ref[...], mb, (((1,), (1,)), ((), ())), preferred_element_type=F32)
    kt_ref[0] = kt.astype(BF16)
    v_ref[0] = _mm(mb, wv_ref[...]).astype(BF16)


def _memkv(mem, wkt, wv):
    bn, m, d = mem.shape
    return pl.pallas_call(
        _memkv_kernel,
        grid=(bn,),
        in_specs=[pl.BlockSpec((1, m, d), lambda i: (i, 0, 0)), _resident((d, d)), _resident((d, d))],
        out_specs=[pl.BlockSpec((1, d, m), lambda i: (i, 0, 0)), pl.BlockSpec((1, m, d), lambda i: (i, 0, 0))],
        out_shape=[jax.ShapeDtypeStruct((bn, d, m), BF16), jax.ShapeDtypeStruct((bn, m, d), BF16)],
        compiler_params=_params(1),
        name="mem_kv",
    )(mem, wkt, wv)


def _cross_kernel(x_ref, kt_ref, v_ref, wq_ref, wo_ref, g_ref, b_ref, o_ref):
    x = x_ref[0]
    d = x.shape[-1]
    hd = d // MEM_HEADS
    q = (_mm(x.astype(BF16), wq_ref[...]) * (hd ** -0.5)).astype(BF16)
    heads = [slice(h * hd, (h + 1) * hd) for h in range(MEM_HEADS)]
    scores = [_mm(q[:, sl], kt_ref[0, sl, :]) for sl in heads]
    outs = []
    for s, sl in zip(scores, heads):
        e = jnp.exp(s - jnp.max(s, axis=-1, keepdims=True))
        p = (e / jnp.sum(e, axis=-1, keepdims=True)).astype(BF16)
        outs.append(_mm(p, v_ref[0, :, sl]))
    o = jnp.concatenate(outs, axis=-1).astype(BF16)
    cross = _mm(o, wo_ref[...])
    o_ref[0] = _ln_rows(DEEPNORM_ALPHA * x + cross, g_ref[...], b_ref[...])


def _cross(x, kt, v, wq, wo, g, b):
    bn, s, d = x.shape
    m = v.shape[1]
    tile = pl.BlockSpec((1, TM_CROSS, d), lambda i, j: (i, j, 0))
    return pl.pallas_call(
        _cross_kernel,
        grid=(bn, s // TM_CROSS),
        in_specs=[tile,
                  pl.BlockSpec((1, d, m), lambda i, j: (i, 0, 0)),
                  pl.BlockSpec((1, m, d), lambda i, j: (i, 0, 0)),
                  _resident((d, d)), _resident((d, d)), _resident((1, d)), _resident((1, d))],
        out_specs=tile,
        out_shape=jax.ShapeDtypeStruct((bn, s, d), F32),
        compiler_params=_params(2),
        name="cross_ln",
    )(x, kt, v, wq, wo, g, b)


def _even_kernel(x_ref, win_ref, lng_ref, lnb_ref, ws_ref, bs_ref, cw_ref, cb_ref, gng_ref, gnb_ref,
                 wout_ref, g_ref, b_ref, o_ref, hbuf_ref, shift_ref):
    tm = x_ref.shape[1]
    aw = A_GROUPS * GROUP_DIM
    bw = B_GROUPS * GROUP_DIM
    x = x_ref[0]
    h = _mm(x.astype(BF16), win_ref[...])
    u = _gelu_exact(h[:, 0:aw])
    v = _gelu_exact(h[:, aw:2 * aw])
    a = h[:, 2 * aw:2 * aw + bw]
    gate = h[:, 2 * aw + bw:]

    row = lax.broadcasted_iota(jnp.int32, (CHUNK, CHUNK), 0)
    col = lax.broadcasted_iota(jnp.int32, (CHUNK, CHUNK), 1)
    causal = row >= col
    y_parts = []
    for g in range(A_GROUPS):
        gs = slice(g * GROUP_DIM, (g + 1) * GROUP_DIM)
        vn = _ln_rows(v[:, gs], lng_ref[:, gs], lnb_ref[:, gs]).astype(BF16)
        wg = jnp.where(causal, ws_ref[g], 0.0).astype(BF16)
        mixed = [_mm(wg, vn[c * CHUNK:(c + 1) * CHUNK, :]) + bs_ref[:, gs]
                 for c in range(tm // CHUNK)]
        y_parts.append(u[:, gs] * jnp.concatenate(mixed, axis=0))

    @pl.when(pl.program_id(1) == 0)
    def _():
        hbuf_ref[0:CONV_HALO, :] = jnp.zeros((CONV_HALO, bw), F32)

    hbuf_ref[CONV_HALO:CONV_HALO + tm, :] = a * jax.nn.sigmoid(gate)
    first = CONV_HALO - (CONV_WIDTH - 1)
    span = tm + CONV_HALO - SUBLANES
    for s in range(1, SUBLANES):
        shift_ref[s - 1, 0:span, :] = hbuf_ref[s:s + span, :]
    strip = 32
    conv_rows = []
    for r0 in range(0, tm, strip):
        acc = jnp.broadcast_to(cb_ref[...], (strip, bw))
        for j in range(CONV_WIDTH):
            s = (first + j) % SUBLANES
            a0 = r0 + (first + j) // SUBLANES * SUBLANES
            src = hbuf_ref[a0:a0 + strip, :] if s == 0 else shift_ref[s - 1, a0:a0 + strip, :]
            acc = acc + cw_ref[j:j + 1, :] * src
        conv_rows.append(acc)
    conv = jnp.concatenate(conv_rows, axis=0)
    hbuf_ref[0:CONV_HALO, :] = hbuf_ref[tm:tm + CONV_HALO, :]
    for g in range(B_GROUPS):
        gs = slice(g * GROUP_DIM, (g + 1) * GROUP_DIM)
        y_parts.append(_silu(_ln_rows(conv[:, gs], gng_ref[:, gs], gnb_ref[:, gs])))

    ycat = jnp.concatenate(y_parts, axis=-1).astype(BF16)
    mix = _mm(ycat, wout_ref[...])
    o_ref[0] = _ln_rows(DEEPNORM_ALPHA * x + mix, g_ref[...], b_ref[...])


def _even_mixer(x, win, lng, lnb, ws, bs_full, cw, cb, gng, gnb, wout, g, b):
    bn, s, d = x.shape
    aw = A_GROUPS * GROUP_DIM
    bw = B_GROUPS * GROUP_DIM
    tile = pl.BlockSpec((1, TM_EVEN, d), lambda i, j: (i, j, 0))
    return pl.pallas_call(
        _even_kernel,
        grid=(bn, s // TM_EVEN),
        in_specs=[tile, _resident(win.shape), _resident((1, aw)), _resident((1, aw)),
                  _resident(ws.shape), _resident(bs_full.shape), _resident(cw.shape), _resident((1, bw)),
                  _resident((1, bw)), _resident((1, bw)), _resident(wout.shape),
                  _resident((1, d)), _resident((1, d))],
        out_specs=tile,
        out_shape=jax.ShapeDtypeStruct((bn, s, d), F32),
        scratch_shapes=[pltpu.VMEM((TM_EVEN + CONV_HALO, bw), F32),
                        pltpu.VMEM((SUBLANES - 1, TM_EVEN + CONV_HALO - SUBLANES, bw), F32)],
        compiler_params=_params(2),
        name="even_mixer_ln",
    )(x, win, lng, lnb, ws, bs_full, cw, cb, gng, gnb, wout, g, b)


def _qkv_kernel(x_ref, w_ref, q_ref, k_ref, v_ref):
    d = x_ref.shape[-1]
    qkv = _mm(x_ref[...].astype(BF16), w_ref[...])
    q_ref[...] = (qkv[:, 0:d] * (C_HEAD_DIM ** -0.5)).astype(BF16)
    k_ref[...] = qkv[:, d:2 * d].astype(BF16)
    v_ref[...] = qkv[:, 2 * d:3 * d].astype(BF16)


def _qkv(x2, w):
    n, d = x2.shape
    tile = pl.BlockSpec((TM_PROJ, d), lambda i: (i, 0))
    out = jax.ShapeDtypeStruct((n, d), BF16)
    return pl.pallas_call(
        _qkv_kernel,
        grid=(n // TM_PROJ,),
        in_specs=[tile, _resident(w.shape)],
        out_specs=[tile, tile, tile],
        out_shape=[out, out, out],
        compiler_params=_params(1),
        name="qkv_proj",
    )(x2, w)


def _stick_kernel(q_ref, k_ref, v_ref, o_ref):
    seq = q_ref.shape[1]
    heads = LANES // C_HEAD_DIM
    rows = heads * Q_BLOCK
    rj = lax.broadcasted_iota(jnp.int32, (K_TILE, K_TILE), 0)
    cs = lax.broadcasted_iota(jnp.int32, (K_TILE, K_TILE), 1)
    from_key_on = (rj >= cs).astype(BF16)
    lane_head = lax.broadcasted_iota(jnp.int32, (rows, LANES), 1) // C_HEAD_DIM
    row_head = lax.broadcasted_iota(jnp.int32, (rows, LANES), 0) // Q_BLOCK
    in_head = lane_head == row_head
    out_lane_head = lax.broadcasted_iota(jnp.int32, (Q_BLOCK, LANES), 1) // C_HEAD_DIM
    key_off = lax.broadcasted_iota(jnp.int32, (rows, K_TILE), 1)
    key_minus_row = key_off - lax.broadcasted_iota(jnp.int32, (rows, K_TILE), 0) % Q_BLOCK

    def tiles(qs, w0s, rel, bounds, carries, accs):
        n = len(qs)
        zs = []
        for q_r, w0 in zip(qs, w0s):
            kt = k_ref[0, pl.ds(w0, K_TILE), :]
            zs.append(lax.dot_general(q_r, kt, (((1,), (1,)), ((), ())), preferred_element_type=F32))
        sps, sums = [], []
        for g0 in range(0, n, SUM_GROUP):
            his, los = [], []
            for r in range(g0, g0 + SUM_GROUP):
                z = jnp.where(rel < bounds[r], zs[r], STICK_MASKED)
                zs[r] = z
                sp = jnp.maximum(z, 0.0) + jnp.log(1.0 + jnp.exp(-jnp.abs(z)))
                hi = sp.astype(BF16)
                sps.append(sp)
                his.append(hi)
                los.append((sp - hi.astype(F32)).astype(BF16))
            sums.append(_mm(jnp.concatenate(his + los, axis=0), from_key_on))
        new_c, new_a = [], []
        for r in range(n):
            s2 = sums[r // SUM_GROUP]
            i = r % SUM_GROUP
            tail = s2[i * rows:(i + 1) * rows] + s2[(SUM_GROUP + i) * rows:(SUM_GROUP + i + 1) * rows]
            logw = zs[r] - tail
            if carries is not None:
                logw = logw - carries[r]
            vt = v_ref[0, pl.ds(w0s[r], K_TILE), :]
            pv = _mm(jnp.exp(logw).astype(BF16), vt)
            rowsum = jnp.sum(sps[r], axis=1, keepdims=True)
            new_a.append(pv if accs is None else accs[r] + pv)
            new_c.append(rowsum if carries is None else carries[r] + rowsum)
        return new_c, new_a

    def q_group(gi, _):
        qs, w_diag, bounds = [], [], []
        for r in range(Q_INFLIGHT):
            t0 = pl.multiple_of((gi * Q_INFLIGHT + r) * Q_BLOCK, Q_BLOCK)
            q2 = q_ref[0, pl.ds(t0, Q_BLOCK), :]
            q2 = jnp.concatenate([q2] * heads, axis=0)
            qs.append(jnp.where(in_head, q2, jnp.zeros_like(q2)))
            w0 = pl.multiple_of(jnp.maximum(t0 + Q_BLOCK - K_TILE, 0), Q_BLOCK)
            w_diag.append(w0)
            bounds.append(t0 - w0)
        carries, accs = tiles(qs, w_diag, key_minus_row, bounds, None, None)

        def cond(st):
            ks, carries, _ = st
            open_carry = [jnp.where(k > 0, c, STICK_EXIT) for k, c in zip(ks, carries)]
            return jnp.min(functools.reduce(jnp.minimum, open_carry)) < STICK_EXIT

        def body(st):
            ks, carries, accs = st
            w0s = [pl.multiple_of(jnp.maximum(k - K_TILE, 0), Q_BLOCK) for k in ks]
            carries, accs = tiles(qs, w0s, key_off, [k - w for k, w in zip(ks, w0s)], carries, accs)
            return w0s, carries, accs

        _, _, accs = lax.while_loop(cond, body, (w_diag, carries, accs))
        outs = []
        for r in range(Q_INFLIGHT):
            out = accs[r][0:Q_BLOCK]
            for hh in range(1, heads):
                out = jnp.where(out_lane_head == hh, accs[r][hh * Q_BLOCK:(hh + 1) * Q_BLOCK], out)
            outs.append(out)
        t0 = pl.multiple_of(gi * Q_INFLIGHT * Q_BLOCK, Q_INFLIGHT * Q_BLOCK)
        o_ref[0, pl.ds(t0, Q_INFLIGHT * Q_BLOCK), :] = jnp.concatenate(outs, axis=0).astype(o_ref.dtype)
        return 0

    lax.fori_loop(0, seq // (Q_BLOCK * Q_INFLIGHT), q_group, 0)


def _stick(q, k, v):
    bn, s, d = q.shape
    blk = pl.BlockSpec((1, s, LANES), lambda i, j: (i, 0, j))
    return pl.pallas_call(
        _stick_kernel,
        grid=(bn, d // LANES),
        in_specs=[blk, blk, blk],
        out_specs=blk,
        out_shape=jax.ShapeDtypeStruct((bn, s, d), BF16),
        compiler_params=_params(2),
        name="stick_breaking",
    )(q, k, v)


def _proj_ln_kernel(y_ref, x_ref, w_ref, g_ref, b_ref, o_ref):
    mix = _mm(y_ref[...], w_ref[...])
    o_ref[...] = _ln_rows(DEEPNORM_ALPHA * x_ref[...] + mix, g_ref[...], b_ref[...])


def _proj_ln(y2, x2, w, g, b):
    n, d = x2.shape
    tile = pl.BlockSpec((TM_PROJ, d), lambda i: (i, 0))
    return pl.pallas_call(
        _proj_ln_kernel,
        grid=(n // TM_PROJ,),
        in_specs=[tile, tile, _resident(w.shape), _resident((1, d)), _resident((1, d))],
        out_specs=tile,
        out_shape=jax.ShapeDtypeStruct((n, d), F32),
        compiler_params=_params(1),
        name="out_proj_ln",
    )(y2, x2, w, g, b)


def kernel(x, mem, w_in_ab, gmlp_ln_g, gmlp_ln_b, gmlp_w_s, gmlp_b_s, conv_w, conv_b, conv_gn_g, conv_gn_b, w_out_ab, w_qkv_c, w_out_c, mem_wq, mem_wk, mem_wv, mem_wo, ffn_w1, ffn_w3, ffn_w2, ln_g, ln_b):
    bn, s, d = x.shape
    n = bn * s
    row = lambda a: a.reshape(1, -1).astype(F32)
    bf = lambda a: a.astype(BF16)
    for layer in range(DEPTH):
        if layer % 2 == 0:
            e = layer // 2
            bs_full = jnp.repeat(gmlp_b_s[e].T, GROUP_DIM, axis=1)
            x = _even_mixer(x, bf(w_in_ab[e]), row(gmlp_ln_g[e]), row(gmlp_ln_b[e]), gmlp_w_s[e], bs_full,
                            conv_w[e], row(conv_b[e]), row(conv_gn_g[e]), row(conv_gn_b[e]), bf(w_out_ab[e]),
                            row(ln_g[layer, 0]), row(ln_b[layer, 0]))
        else:
            o = layer // 2
            x2 = x.reshape(n, d)
            q, k, v = _qkv(x2, bf(w_qkv_c[o]))
            y = _stick(q.reshape(bn, s, d), k.reshape(bn, s, d), v.reshape(bn, s, d))
            x = _proj_ln(y.reshape(n, d), x2, bf(w_out_c[o]), row(ln_g[layer, 0]), row(ln_b[layer, 0])).reshape(bn, s, d)
        kt, vm = _memkv(mem, bf(mem_wk[layer].T), bf(mem_wv[layer]))
        x = _cross(x, kt, vm, bf(mem_wq[layer]), bf(mem_wo[layer]), row(ln_g[layer, 1]), row(ln_b[layer, 1]))
        x = _ffn(x.reshape(n, d), bf(ffn_w1[layer]), bf(ffn_w3[layer]), bf(ffn_w2[layer]),
                 row(ln_g[layer, 2]), row(ln_b[layer, 2])).reshape(bn, s, d)
    return x
```

```python
import functools

import jax
import jax.numpy as jnp
from jax import lax
from jax.experimental import pallas as pl
from jax.experimental.pallas import tpu as pltpu

F32 = jnp.float32
BF16 = jnp.bfloat16

DEPTH = 4
CHUNK = 128
A_GROUPS = 4
B_GROUPS = 4
GROUP_DIM = 128
CONV_WIDTH = 31
C_HEADS = 16
C_HEAD_DIM = 64
MEM_HEADS = 4
DEEPNORM_ALPHA = (2.0 * DEPTH) ** 0.25
LN_EPS = 1e-5

LANES = 128
SUBLANES = 8
CONV_HALO = 32
Q_BLOCK = 64
K_TILE = 256
STICK_EXIT = 88.0
STICK_MASKED = -1e30
SUM_GROUP = 4
Q_INFLIGHT = 8

TM_FFN = 512
TM_CROSS = 1024
TM_EVEN = 512
TM_PROJ = 1024
VMEM_LIMIT = 56 * 1024 * 1024


def _resident(shape):
    nd = len(shape)
    return pl.BlockSpec(shape, lambda *_: (0,) * nd, pipeline_mode=pl.Buffered(1))


def _params(n_axes):
    return pltpu.CompilerParams(dimension_semantics=("arbitrary",) * n_axes,
                                vmem_limit_bytes=VMEM_LIMIT)


def _ln_rows(y, g, b):
    mu = jnp.mean(y, axis=-1, keepdims=True)
    yc = y - mu
    var = jnp.mean(yc * yc, axis=-1, keepdims=True)
    return yc * lax.rsqrt(var + LN_EPS) * g + b


def _gelu_exact(x):
    return 0.5 * x * (1.0 + lax.erf(x * (0.5 ** 0.5)))


def _silu(x):
    return x * jax.nn.sigmoid(x)


def _mm(a, b):
    return jnp.dot(a, b, preferred_element_type=F32)


def _ffn_kernel(x_ref, w1_ref, w3_ref, w2_ref, g_ref, b_ref, o_ref):
    x = x_ref[...]
    xb = x.astype(BF16)
    h1 = _mm(xb, w1_ref[...])
    h3 = _mm(xb, w3_ref[...])
    act = (_silu(h1) * h3).astype(BF16)
    y = _mm(act, w2_ref[...])
    o_ref[...] = _ln_rows(DEEPNORM_ALPHA * x + y, g_ref[...], b_ref[...])


def _ffn(x2, w1, w3, w2, g, b):
    n, d = x2.shape
    dff = w1.shape[1]
    tile = pl.BlockSpec((TM_FFN, d), lambda i: (i, 0))
    return pl.pallas_call(
        _ffn_kernel,
        grid=(n // TM_FFN,),
        in_specs=[tile, _resident((d, dff)), _resident((d, dff)), _resident((dff, d)),
                  _resident((1, d)), _resident((1, d))],
        out_specs=tile,
        out_shape=jax.ShapeDtypeStruct((n, d), F32),
        compiler_params=_params(1),
        name="ffn_ln",
    )(x2, w1, w3, w2, g, b)


def _memkv_kernel(mem_ref, wkt_ref, wv_ref, kt_ref, v_ref):
    mb = mem_ref[0].astype(BF16)
    kt = lax.dot_general(wkt_ref[...], mb, (((1,), (1,)), ((), ())), preferred_element_type=F32)
    kt_ref[0] = kt.astype(BF16)
    v_ref[0] = _mm(mb, wv_ref[...]).astype(BF16)


def _memkv(mem, wkt, wv):
    bn, m, d = mem.shape
    return pl.pallas_call(
        _memkv_kernel,
        grid=(bn,),
        in_specs=[pl.BlockSpec((1, m, d), lambda i: (i, 0, 0)), _resident((d, d)), _resident((d, d))],
        out_specs=[pl.BlockSpec((1, d, m), lambda i: (i, 0, 0)), pl.BlockSpec((1, m, d), lambda i: (i, 0, 0))],
        out_shape=[jax.ShapeDtypeStruct((bn, d, m), BF16), jax.ShapeDtypeStruct((bn, m, d), BF16)],
        compiler_params=_params(1),
        name="mem_kv",
    )(mem, wkt, wv)


def _cross_kernel(x_ref, kt_ref, v_ref, wq_ref, wo_ref, g_ref, b_ref, o_ref):
    x = x_ref[0]
    d = x.shape[-1]
    hd = d // MEM_HEADS
    q = (_mm(x.astype(BF16), wq_ref[...]) * (hd ** -0.5)).astype(BF16)
    heads = [slice(h * hd, (h + 1) * hd) for h in range(MEM_HEADS)]
    scores = [_mm(q[:, sl], kt_ref[0, sl, :]) for sl in heads]
    outs = []
    for s, sl in zip(scores, heads):
        e = jnp.exp(s - jnp.max(s, axis=-1, keepdims=True))
        p = (e / jnp.sum(e, axis=-1, keepdims=True)).astype(BF16)
        outs.append(_mm(p, v_ref[0, :, sl]))
    o = jnp.concatenate(outs, axis=-1).astype(BF16)
    cross = _mm(o, wo_ref[...])
    o_ref[0] = _ln_rows(DEEPNORM_ALPHA * x + cross, g_ref[...], b_ref[...])


def _cross(x, kt, v, wq, wo, g, b):
    bn, s, d = x.shape
    m = v.shape[1]
    tile = pl.BlockSpec((1, TM_CROSS, d), lambda i, j: (i, j, 0))
    return pl.pallas_call(
        _cross_kernel,
        grid=(bn, s // TM_CROSS),
        in_specs=[tile,
                  pl.BlockSpec((1, d, m), lambda i, j: (i, 0, 0)),
                  pl.BlockSpec((1, m, d), lambda i, j: (i, 0, 0)),
                  _resident((d, d)), _resident((d, d)), _resident((1, d)), _resident((1, d))],
        out_specs=tile,
        out_shape=jax.ShapeDtypeStruct((bn, s, d), F32),
        compiler_params=_params(2),
        name="cross_ln",
    )(x, kt, v, wq, wo, g, b)


def _even_kernel(x_ref, win_ref, lng_ref, lnb_ref, ws_ref, bs_ref, cw_ref, cb_ref, gng_ref, gnb_ref,
                 wout_ref, g_ref, b_ref, o_ref, hbuf_ref, shift_ref):
    tm = x_ref.shape[1]
    aw = A_GROUPS * GROUP_DIM
    bw = B_GROUPS * GROUP_DIM
    x = x_ref[0]
    h = _mm(x.astype(BF16), win_ref[...])
    u = _gelu_exact(h[:, 0:aw])
    v = _gelu_exact(h[:, aw:2 * aw])
    a = h[:, 2 * aw:2 * aw + bw]
    gate = h[:, 2 * aw + bw:]

    row = lax.broadcasted_iota(jnp.int32, (CHUNK, CHUNK), 0)
    col = lax.broadcasted_iota(jnp.int32, (CHUNK, CHUNK), 1)
    causal = row >= col
    y_parts = []
    for g in range(A_GROUPS):
        gs = slice(g * GROUP_DIM, (g + 1) * GROUP_DIM)
        vn = _ln_rows(v[:, gs], lng_ref[:, gs], lnb_ref[:, gs]).astype(BF16)
        wg = jnp.where(causal, ws_ref[g], 0.0).astype(BF16)
        mixed = [_mm(wg, vn[c * CHUNK:(c + 1) * CHUNK, :]) + bs_ref[:, gs]
                 for c in range(tm // CHUNK)]
        y_parts.append(u[:, gs] * jnp.concatenate(mixed, axis=0))

    @pl.when(pl.program_id(1) == 0)
    def _():
        hbuf_ref[0:CONV_HALO, :] = jnp.zeros((CONV_HALO, bw), F32)

    hbuf_ref[CONV_HALO:CONV_HALO + tm, :] = a * jax.nn.sigmoid(gate)
    first = CONV_HALO - (CONV_WIDTH - 1)
    span = tm + CONV_HALO - SUBLANES
    for s in range(1, SUBLANES):
        shift_ref[s - 1, 0:span, :] = hbuf_ref[s:s + span, :]
    strip = 32
    conv_rows = []
    for r0 in range(0, tm, strip):
        acc = jnp.broadcast_to(cb_ref[...], (strip, bw))
        for j in range(CONV_WIDTH):
            s = (first + j) % SUBLANES
            a0 = r0 + (first + j) // SUBLANES * SUBLANES
            src = hbuf_ref[a0:a0 + strip, :] if s == 0 else shift_ref[s - 1, a0:a0 + strip, :]
            acc = acc + cw_ref[j:j + 1, :] * src
        conv_rows.append(acc)
    conv = jnp.concatenate(conv_rows, axis=0)
    hbuf_ref[0:CONV_HALO, :] = hbuf_ref[tm:tm + CONV_HALO, :]
    for g in range(B_GROUPS):
        gs = slice(g * GROUP_DIM, (g + 1) * GROUP_DIM)
        y_parts.append(_silu(_ln_rows(conv[:, gs], gng_ref[:, gs], gnb_ref[:, gs])))

    ycat = jnp.concatenate(y_parts, axis=-1).astype(BF16)
    mix = _mm(ycat, wout_ref[...])
    o_ref[0] = _ln_rows(DEEPNORM_ALPHA * x + mix, g_ref[...], b_ref[...])


def _even_mixer(x, win, lng, lnb, ws, bs_full, cw, cb, gng, gnb, wout, g, b):
    bn, s, d = x.shape
    aw = A_GROUPS * GROUP_DIM
    bw = B_GROUPS * GROUP_DIM
    tile = pl.BlockSpec((1, TM_EVEN, d), lambda i, j: (i, j, 0))
    return pl.pallas_call(
        _even_kernel,
        grid=(bn, s // TM_EVEN),
        in_specs=[tile, _resident(win.shape), _resident((1, aw)), _resident((1, aw)),
                  _resident(ws.shape), _resident(bs_full.shape), _resident(cw.shape), _resident((1, bw)),
                  _resident((1, bw)), _resident((1, bw)), _resident(wout.shape),
                  _resident((1, d)), _resident((1, d))],
        out_specs=tile,
        out_shape=jax.ShapeDtypeStruct((bn, s, d), F32),
        scratch_shapes=[pltpu.VMEM((TM_EVEN + CONV_HALO, bw), F32),
                        pltpu.VMEM((SUBLANES - 1, TM_EVEN + CONV_HALO - SUBLANES, bw), F32)],
        compiler_params=_params(2),
        name="even_mixer_ln",
    )(x, win, lng, lnb, ws, bs_full, cw, cb, gng, gnb, wout, g, b)


def _qkv_kernel(x_ref, w_ref, q_ref, k_ref, v_ref):
    d = x_ref.shape[-1]
    qkv = _mm(x_ref[...].astype(BF16), w_ref[...])
    q_ref[...] = (qkv[:, 0:d] * (C_HEAD_DIM ** -0.5)).astype(BF16)
    k_ref[...] = qkv[:, d:2 * d].astype(BF16)
    v_ref[...] = qkv[:, 2 * d:3 * d].astype(BF16)


def _qkv(x2, w):
    n, d = x2.shape
    tile = pl.BlockSpec((TM_PROJ, d), lambda i: (i, 0))
    out = jax.ShapeDtypeStruct((n, d), BF16)
    return pl.pallas_call(
        _qkv_kernel,
        grid=(n // TM_PROJ,),
        in_specs=[tile, _resident(w.shape)],
        out_specs=[tile, tile, tile],
        out_shape=[out, out, out],
        compiler_params=_params(1),
        name="qkv_proj",
    )(x2, w)


def _stick_kernel(q_ref, k_ref, v_ref, o_ref):
    seq = q_ref.shape[1]
    heads = LANES // C_HEAD_DIM
    rows = heads * Q_BLOCK
    rj = lax.broadcasted_iota(jnp.int32, (K_TILE, K_TILE), 0)
    cs = lax.broadcasted_iota(jnp.int32, (K_TILE, K_TILE), 1)
    from_key_on = (rj >= cs).astype(BF16)
    lane_head = lax.broadcasted_iota(jnp.int32, (rows, LANES), 1) // C_HEAD_DIM
    row_head = lax.broadcasted_iota(jnp.int32, (rows, LANES), 0) // Q_BLOCK
    in_head = lane_head == row_head
    out_lane_head = lax.broadcasted_iota(jnp.int32, (Q_BLOCK, LANES), 1) // C_HEAD_DIM
    key_off = lax.broadcasted_iota(jnp.int32, (rows, K_TILE), 1)
    key_minus_row = key_off - lax.broadcasted_iota(jnp.int32, (rows, K_TILE), 0) % Q_BLOCK

    def tiles(qs, w0s, rel, bounds, carries, accs):
        n = len(qs)
        zs = []
        for q_r, w0 in zip(qs, w0s):
            kt = k_ref[0, pl.ds(w0, K_TILE), :]
            zs.append(lax.dot_general(q_r, kt, (((1,), (1,)), ((), ())), preferred_element_type=F32))
        sps, sums = [], []
        for g0 in range(0, n, SUM_GROUP):
            his, los = [], []
            for r in range(g0, g0 + SUM_GROUP):
                z = jnp.where(rel < bounds[r], zs[r], STICK_MASKED)
                zs[r] = z
                sp = jnp.maximum(z, 0.0) + jnp.log(1.0 + jnp.exp(-jnp.abs(z)))
                hi = sp.astype(BF16)
                sps.append(sp)
                his.append(hi)
                los.append((sp - hi.astype(F32)).astype(BF16))
            sums.append(_mm(jnp.concatenate(his + los, axis=0), from_key_on))
        new_c, new_a = [], []
        for r in range(n):
            s2 = sums[r // SUM_GROUP]
            i = r % SUM_GROUP
            tail = s2[i * rows:(i + 1) * rows] + s2[(SUM_GROUP + i) * rows:(SUM_GROUP + i + 1) * rows]
            logw = zs[r] - tail
            if carries is not None:
                logw = logw - carries[r]
            vt = v_ref[0, pl.ds(w0s[r], K_TILE), :]
            pv = _mm(jnp.exp(logw).astype(BF16), vt)
            rowsum = jnp.sum(sps[r], axis=1, keepdims=True)
            new_a.append(pv if accs is None else accs[r] + pv)
            new_c.append(rowsum if carries is None else carries[r] + rowsum)
        return new_c, new_a

    def diagonal_tiles(gi):
        qs, w_diag, bounds = [], [], []
        for r in range(Q_INFLIGHT):
            t0 = pl.multiple_of((gi * Q_INFLIGHT + r) * Q_BLOCK, Q_BLOCK)
            q2 = q_ref[0, pl.ds(t0, Q_BLOCK), :]
            q2 = jnp.concatenate([q2] * heads, axis=0)
            qs.append(jnp.where(in_head, q2, jnp.zeros_like(q2)))
            w0 = pl.multiple_of(jnp.maximum(t0 + Q_BLOCK - K_TILE, 0), Q_BLOCK)
            w_diag.append(w0)
            bounds.append(t0 - w0)
        carries, accs = tiles(qs, w_diag, key_minus_row, bounds, None, None)
        return qs, w_diag, carries, accs

    def open_carry(ks, carries):
        return functools.reduce(jnp.minimum, [jnp.where(k > 0, c, STICK_EXIT) for k, c in zip(ks, carries)])

    def write_out(gi, accs):
        outs = []
        for r in range(Q_INFLIGHT):
            out = accs[r][0:Q_BLOCK]
            for hh in range(1, heads):
                out = jnp.where(out_lane_head == hh, accs[r][hh * Q_BLOCK:(hh + 1) * Q_BLOCK], out)
            outs.append(out)
        t0 = pl.multiple_of(gi * Q_INFLIGHT * Q_BLOCK, Q_INFLIGHT * Q_BLOCK)
        o_ref[0, pl.ds(t0, Q_INFLIGHT * Q_BLOCK), :] = jnp.concatenate(outs, axis=0).astype(o_ref.dtype)

    def walk_older_keys(gi):
        qs, w_diag, carries, accs = diagonal_tiles(gi)

        def cond(st):
            ks, carries, _ = st
            return jnp.min(open_carry(ks, carries)) < STICK_EXIT

        def body(st):
            ks, carries, accs = st
            w0s = [pl.multiple_of(jnp.maximum(k - K_TILE, 0), Q_BLOCK) for k in ks]
            carries, accs = tiles(qs, w0s, key_off, [k - w for k, w in zip(ks, w0s)], carries, accs)
            return w0s, carries, accs

        _, _, accs = lax.while_loop(cond, body, (w_diag, carries, accs))
        write_out(gi, accs)

    n_groups = seq // (Q_BLOCK * Q_INFLIGHT)

    def q_group(gi, prev_open):
        prev_unfinished = jnp.min(prev_open) < STICK_EXIT
        _, w_diag, carries, accs = diagonal_tiles(gi)
        write_out(gi, accs)

        @pl.when(prev_unfinished)
        def _():
            walk_older_keys(jnp.maximum(gi - 1, 0))

        return open_carry(w_diag, carries)

    last_open = lax.fori_loop(0, n_groups, q_group, jnp.full((rows, 1), STICK_EXIT, F32))

    @pl.when(jnp.min(last_open) < STICK_EXIT)
    def _():
        walk_older_keys(n_groups - 1)


def _stick(q, k, v):
    bn, s, d = q.shape
    blk = pl.BlockSpec((1, s, LANES), lambda i, j: (i, 0, j))
    return pl.pallas_call(
        _stick_kernel,
        grid=(bn, d // LANES),
        in_specs=[blk, blk, blk],
        out_specs=blk,
        out_shape=jax.ShapeDtypeStruct((bn, s, d), BF16),
        compiler_params=_params(2),
        name="stick_breaking",
    )(q, k, v)


def _proj_ln_kernel(y_ref, x_ref, w_ref, g_ref, b_ref, o_ref):
    mix = _mm(y_ref[...], w_ref[...])
    o_ref[...] = _ln_rows(DEEPNORM_ALPHA * x_ref[...] + mix, g_ref[...], b_ref[...])


def _proj_ln(y2, x2, w, g, b):
    n, d = x2.shape
    tile = pl.BlockSpec((TM_PROJ, d), lambda i: (i, 0))
    return pl.pallas_call(
        _proj_ln_kernel,
        grid=(n // TM_PROJ,),
        in_specs=[tile, tile, _resident(w.shape), _resident((1, d)), _resident((1, d))],
        out_specs=tile,
        out_shape=jax.ShapeDtypeStruct((n, d), F32),
        compiler_params=_params(1),
        name="out_proj_ln",
    )(y2, x2, w, g, b)


def kernel(x, mem, w_in_ab, gmlp_ln_g, gmlp_ln_b, gmlp_w_s, gmlp_b_s, conv_w, conv_b, conv_gn_g, conv_gn_b, w_out_ab, w_qkv_c, w_out_c, mem_wq, mem_wk, mem_wv, mem_wo, ffn_w1, ffn_w3, ffn_w2, ln_g, ln_b):
    bn, s, d = x.shape
    n = bn * s
    row = lambda a: a.reshape(1, -1).astype(F32)
    bf = lambda a: a.astype(BF16)
    for layer in range(DEPTH):
        if layer % 2 == 0:
            e = layer // 2
            bs_full = jnp.repeat(gmlp_b_s[e].T, GROUP_DIM, axis=1)
            x = _even_mixer(x, bf(w_in_ab[e]), row(gmlp_ln_g[e]), row(gmlp_ln_b[e]), gmlp_w_s[e], bs_full,
                            conv_w[e], row(conv_b[e]), row(conv_gn_g[e]), row(conv_gn_b[e]), bf(w_out_ab[e]),
                            row(ln_g[layer, 0]), row(ln_b[layer, 0]))
        else:
            o = layer // 2
            x2 = x.reshape(n, d)
            q, k, v = _qkv(x2, bf(w_qkv_c[o]))
            y = _stick(q.reshape(bn, s, d), k.reshape(bn, s, d), v.reshape(bn, s, d))
            x = _proj_ln(y.reshape(n, d), x2, bf(w_out_c[o]), row(ln_g[layer, 0]), row(ln_b[layer, 0])).reshape(bn, s, d)
        kt, vm = _memkv(mem, bf(mem_wk[layer].T), bf(mem_wv[layer]))
        x = _cross(x, kt, vm, bf(mem_wq[layer]), bf(mem_wo[layer]), row(ln_g[layer, 1]), row(ln_b[layer, 1]))
        x = _ffn(x.reshape(n, d), bf(ffn_w1[layer]), bf(ffn_w3[layer]), bf(ffn_w2[layer]),
                 row(ln_g[layer, 2]), row(ln_b[layer, 2])).reshape(bn, s, d)
    return x
```

```python
import functools

import jax
import jax.numpy as jnp
from jax import lax
from jax.experimental import pallas as pl
from jax.experimental.pallas import tpu as pltpu

F32 = jnp.float32
BF16 = jnp.bfloat16

DEPTH = 4
CHUNK = 128
A_GROUPS = 4
B_GROUPS = 4
GROUP_DIM = 128
CONV_WIDTH = 31
C_HEADS = 16
C_HEAD_DIM = 64
MEM_HEADS = 4
DEEPNORM_ALPHA = (2.0 * DEPTH) ** 0.25
LN_EPS = 1e-5

LANES = 128
SUBLANES = 8
CONV_HALO = 32
Q_BLOCK = 64
K_TILE = 256
STICK_EXIT = 88.0
STICK_MASKED = -1e30
SUM_GROUP = 4
Q_INFLIGHT = 16

TM_FFN = 512
TM_CROSS = 1024
TM_EVEN = 1024
TM_PROJ = 1024
VMEM_LIMIT = 56 * 1024 * 1024


def _resident(shape):
    nd = len(shape)
    return pl.BlockSpec(shape, lambda *_: (0,) * nd, pipeline_mode=pl.Buffered(1))


def _params(n_axes):
    return pltpu.CompilerParams(dimension_semantics=("arbitrary",) * n_axes,
                                vmem_limit_bytes=VMEM_LIMIT)


def _ln_rows(y, g, b):
    mu = jnp.mean(y, axis=-1, keepdims=True)
    yc = y - mu
    var = jnp.mean(yc * yc, axis=-1, keepdims=True)
    return yc * lax.rsqrt(var + LN_EPS) * g + b


def _gelu_exact(x):
    return 0.5 * x * (1.0 + lax.erf(x * (0.5 ** 0.5)))


def _silu(x):
    return x * jax.nn.sigmoid(x)


def _mm(a, b):
    return jnp.dot(a, b, preferred_element_type=F32)


def _ffn_kernel(x_ref, w1_ref, w3_ref, w2_ref, g_ref, b_ref, o_ref):
    x = x_ref[...]
    xb = x.astype(BF16)
    h1 = _mm(xb, w1_ref[...])
    h3 = _mm(xb, w3_ref[...])
    act = (_silu(h1) * h3).astype(BF16)
    y = _mm(act, w2_ref[...])
    o_ref[...] = _ln_rows(DEEPNORM_ALPHA * x + y, g_ref[...], b_ref[...])


def _ffn(x2, w1, w3, w2, g, b):
    n, d = x2.shape
    dff = w1.shape[1]
    tile = pl.BlockSpec((TM_FFN, d), lambda i: (i, 0))
    return pl.pallas_call(
        _ffn_kernel,
        grid=(n // TM_FFN,),
        in_specs=[tile, _resident((d, dff)), _resident((d, dff)), _resident((dff, d)),
                  _resident((1, d)), _resident((1, d))],
        out_specs=tile,
        out_shape=jax.ShapeDtypeStruct((n, d), F32),
        compiler_params=_params(1),
        name="ffn_ln",
    )(x2, w1, w3, w2, g, b)


def _memkv_kernel(mem_ref, wkt_ref, wv_ref, kt_ref, v_ref):
    mb = mem_ref[0].astype(BF16)
    kt = lax.dot_general(wkt_ref[...], mb, (((1,), (1,)), ((), ())), preferred_element_type=F32)
    kt_ref[0] = kt.astype(BF16)
    v_ref[0] = _mm(mb, wv_ref[...]).astype(BF16)


def _memkv(mem, wkt, wv):
    bn, m, d = mem.shape
    return pl.pallas_call(
        _memkv_kernel,
        grid=(bn,),
        in_specs=[pl.BlockSpec((1, m, d), lambda i: (i, 0, 0)), _resident((d, d)), _resident((d, d))],
        out_specs=[pl.BlockSpec((1, d, m), lambda i: (i, 0, 0)), pl.BlockSpec((1, m, d), lambda i: (i, 0, 0))],
        out_shape=[jax.ShapeDtypeStruct((bn, d, m), BF16), jax.ShapeDtypeStruct((bn, m, d), BF16)],
        compiler_params=_params(1),
        name="mem_kv",
    )(mem, wkt, wv)


def _cross_kernel(x_ref, kt_ref, v_ref, wq_ref, wo_ref, g_ref, b_ref, o_ref):
    x = x_ref[0]
    d = x.shape[-1]
    hd = d // MEM_HEADS
    q = (_mm(x.astype(BF16), wq_ref[...]) * (hd ** -0.5)).astype(BF16)
    heads = [slice(h * hd, (h + 1) * hd) for h in range(MEM_HEADS)]
    scores = [_mm(q[:, sl], kt_ref[0, sl, :]) for sl in heads]
    outs = []
    for s, sl in zip(scores, heads):
        e = jnp.exp(s - jnp.max(s, axis=-1, keepdims=True))
        p = (e / jnp.sum(e, axis=-1, keepdims=True)).astype(BF16)
        outs.append(_mm(p, v_ref[0, :, sl]))
    o = jnp.concatenate(outs, axis=-1).astype(BF16)
    cross = _mm(o, wo_ref[...])
    o_ref[0] = _ln_rows(DEEPNORM_ALPHA * x + cross, g_ref[...], b_ref[...])


def _cross(x, kt, v, wq, wo, g, b):
    bn, s, d = x.shape
    m = v.shape[1]
    tile = pl.BlockSpec((1, TM_CROSS, d), lambda i, j: (i, j, 0))
    return pl.pallas_call(
        _cross_kernel,
        grid=(bn, s // TM_CROSS),
        in_specs=[tile,
                  pl.BlockSpec((1, d, m), lambda i, j: (i, 0, 0)),
                  pl.BlockSpec((1, m, d), lambda i, j: (i, 0, 0)),
                  _resident((d, d)), _resident((d, d)), _resident((1, d)), _resident((1, d))],
        out_specs=tile,
        out_shape=jax.ShapeDtypeStruct((bn, s, d), F32),
        compiler_params=_params(2),
        name="cross_ln",
    )(x, kt, v, wq, wo, g, b)


def _even_kernel(x_ref, win_ref, lng_ref, lnb_ref, ws_ref, bs_ref, cw_ref, cb_ref, gng_ref, gnb_ref,
                 wout_ref, g_ref, b_ref, o_ref, hbuf_ref, shift_ref):
    tm = x_ref.shape[1]
    aw = A_GROUPS * GROUP_DIM
    bw = B_GROUPS * GROUP_DIM
    x = x_ref[0]
    h = _mm(x.astype(BF16), win_ref[...])
    u = _gelu_exact(h[:, 0:aw])
    v = _gelu_exact(h[:, aw:2 * aw])
    a = h[:, 2 * aw:2 * aw + bw]
    gate = h[:, 2 * aw + bw:]

    row = lax.broadcasted_iota(jnp.int32, (CHUNK, CHUNK), 0)
    col = lax.broadcasted_iota(jnp.int32, (CHUNK, CHUNK), 1)
    causal = row >= col
    y_parts = []
    for g in range(A_GROUPS):
        gs = slice(g * GROUP_DIM, (g + 1) * GROUP_DIM)
        vn = _ln_rows(v[:, gs], lng_ref[:, gs], lnb_ref[:, gs]).astype(BF16)
        wg = jnp.where(causal, ws_ref[g], 0.0).astype(BF16)
        mixed = [_mm(wg, vn[c * CHUNK:(c + 1) * CHUNK, :]) + bs_ref[:, gs]
                 for c in range(tm // CHUNK)]
        y_parts.append(u[:, gs] * jnp.concatenate(mixed, axis=0))

    @pl.when(pl.program_id(1) == 0)
    def _():
        hbuf_ref[0:CONV_HALO, :] = jnp.zeros((CONV_HALO, bw), F32)

    hbuf_ref[CONV_HALO:CONV_HALO + tm, :] = a * jax.nn.sigmoid(gate)
    first = CONV_HALO - (CONV_WIDTH - 1)
    span = tm + CONV_HALO - SUBLANES
    for s in range(1, SUBLANES):
        shift_ref[s - 1, 0:span, :] = hbuf_ref[s:s + span, :]
    strip = 32
    conv_rows = []
    for r0 in range(0, tm, strip):
        acc = jnp.broadcast_to(cb_ref[...], (strip, bw))
        for j in range(CONV_WIDTH):
            s = (first + j) % SUBLANES
            a0 = r0 + (first + j) // SUBLANES * SUBLANES
            src = hbuf_ref[a0:a0 + strip, :] if s == 0 else shift_ref[s - 1, a0:a0 + strip, :]
            acc = acc + cw_ref[j:j + 1, :] * src
        conv_rows.append(acc)
    conv = jnp.concatenate(conv_rows, axis=0)
    hbuf_ref[0:CONV_HALO, :] = hbuf_ref[tm:tm + CONV_HALO, :]
    for g in range(B_GROUPS):
        gs = slice(g * GROUP_DIM, (g + 1) * GROUP_DIM)
        y_parts.append(_silu(_ln_rows(conv[:, gs], gng_ref[:, gs], gnb_ref[:, gs])))

    ycat = jnp.concatenate(y_parts, axis=-1).astype(BF16)
    mix = _mm(ycat, wout_ref[...])
    o_ref[0] = _ln_rows(DEEPNORM_ALPHA * x + mix, g_ref[...], b_ref[...])


def _even_mixer(x, win, lng, lnb, ws, bs_full, cw, cb, gng, gnb, wout, g, b):
    bn, s, d = x.shape
    aw = A_GROUPS * GROUP_DIM
    bw = B_GROUPS * GROUP_DIM
    tile = pl.BlockSpec((1, TM_EVEN, d), lambda i, j: (i, j, 0))
    return pl.pallas_call(
        _even_kernel,
        grid=(bn, s // TM_EVEN),
        in_specs=[tile, _resident(win.shape), _resident((1, aw)), _resident((1, aw)),
                  _resident(ws.shape), _resident(bs_full.shape), _resident(cw.shape), _resident((1, bw)),
                  _resident((1, bw)), _resident((1, bw)), _resident(wout.shape),
                  _resident((1, d)), _resident((1, d))],
        out_specs=tile,
        out_shape=jax.ShapeDtypeStruct((bn, s, d), F32),
        scratch_shapes=[pltpu.VMEM((TM_EVEN + CONV_HALO, bw), F32),
                        pltpu.VMEM((SUBLANES - 1, TM_EVEN + CONV_HALO - SUBLANES, bw), F32)],
        compiler_params=_params(2),
        name="even_mixer_ln",
    )(x, win, lng, lnb, ws, bs_full, cw, cb, gng, gnb, wout, g, b)


def _qkv_kernel(x_ref, w_ref, q_ref, k_ref, v_ref):
    d = x_ref.shape[-1]
    qkv = _mm(x_ref[...].astype(BF16), w_ref[...])
    q_ref[...] = (qkv[:, 0:d] * (C_HEAD_DIM ** -0.5)).astype(BF16)
    k_ref[...] = qkv[:, d:2 * d].astype(BF16)
    v_ref[...] = qkv[:, 2 * d:3 * d].astype(BF16)


def _qkv(x2, w):
    n, d = x2.shape
    tile = pl.BlockSpec((TM_PROJ, d), lambda i: (i, 0))
    out = jax.ShapeDtypeStruct((n, d), BF16)
    return pl.pallas_call(
        _qkv_kernel,
        grid=(n // TM_PROJ,),
        in_specs=[tile, _resident(w.shape)],
        out_specs=[tile, tile, tile],
        out_shape=[out, out, out],
        compiler_params=_params(1),
        name="qkv_proj",
    )(x2, w)


def _stick_kernel(q_ref, k_ref, v_ref, o_ref):
    seq = q_ref.shape[1]
    heads = LANES // C_HEAD_DIM
    rows = heads * Q_BLOCK
    rj = lax.broadcasted_iota(jnp.int32, (K_TILE, K_TILE), 0)
    cs = lax.broadcasted_iota(jnp.int32, (K_TILE, K_TILE), 1)
    from_key_on = (rj >= cs).astype(BF16)
    lane_head = lax.broadcasted_iota(jnp.int32, (rows, LANES), 1) // C_HEAD_DIM
    row_head = lax.broadcasted_iota(jnp.int32, (rows, LANES), 0) // Q_BLOCK
    in_head = lane_head == row_head
    out_lane_head = lax.broadcasted_iota(jnp.int32, (Q_BLOCK, LANES), 1) // C_HEAD_DIM
    key_off = lax.broadcasted_iota(jnp.int32, (rows, K_TILE), 1)
    key_minus_row = key_off - lax.broadcasted_iota(jnp.int32, (rows, K_TILE), 0) % Q_BLOCK

    def tiles(qs, w0s, rel, bounds, carries, accs):
        n = len(qs)
        zs = []
        for q_r, w0 in zip(qs, w0s):
            kt = k_ref[0, pl.ds(w0, K_TILE), :]
            zs.append(lax.dot_general(q_r, kt, (((1,), (1,)), ((), ())), preferred_element_type=F32))
        sps, sums = [], []
        for g0 in range(0, n, SUM_GROUP):
            his, los = [], []
            for r in range(g0, g0 + SUM_GROUP):
                z = jnp.where(rel < bounds[r], zs[r], STICK_MASKED)
                zs[r] = z
                neg_abs = pltpu.bitcast(pltpu.bitcast(z, jnp.int32) | jnp.int32(-2 ** 31), F32)
                sp = jnp.maximum(z, 0.0) + jnp.log(1.0 + jnp.exp(neg_abs))
                hi = sp.astype(BF16)
                sps.append(sp)
                his.append(hi)
                los.append((sp - hi.astype(F32)).astype(BF16))
            sums.append(_mm(jnp.concatenate(his + los, axis=0), from_key_on))
        new_c, new_a = [], []
        for r in range(n):
            s2 = sums[r // SUM_GROUP]
            i = r % SUM_GROUP
            tail = s2[i * rows:(i + 1) * rows] + s2[(SUM_GROUP + i) * rows:(SUM_GROUP + i + 1) * rows]
            logw = zs[r] - tail
            if carries is not None:
                logw = logw - carries[r]
            vt = v_ref[0, pl.ds(w0s[r], K_TILE), :]
            pv = _mm(jnp.exp(logw).astype(BF16), vt)
            rowsum = jnp.sum(sps[r], axis=1, keepdims=True)
            new_a.append(pv if accs is None else accs[r] + pv)
            new_c.append(rowsum if carries is None else carries[r] + rowsum)
        return new_c, new_a

    def diagonal_tiles(gi):
        qs, w_diag, bounds = [], [], []
        for r in range(Q_INFLIGHT):
            t0 = pl.multiple_of((gi * Q_INFLIGHT + r) * Q_BLOCK, Q_BLOCK)
            q2 = q_ref[0, pl.ds(t0, Q_BLOCK), :]
            q2 = jnp.concatenate([q2] * heads, axis=0)
            qs.append(jnp.where(in_head, q2, jnp.zeros_like(q2)))
            w0 = pl.multiple_of(jnp.maximum(t0 + Q_BLOCK - K_TILE, 0), Q_BLOCK)
            w_diag.append(w0)
            bounds.append(t0 - w0)
        carries, accs = tiles(qs, w_diag, key_minus_row, bounds, None, None)
        return qs, w_diag, carries, accs

    def open_carry(ks, carries):
        return functools.reduce(jnp.minimum, [jnp.where(k > 0, c, STICK_EXIT) for k, c in zip(ks, carries)])

    def write_out(gi, accs):
        outs = []
        for r in range(Q_INFLIGHT):
            out = accs[r][0:Q_BLOCK]
            for hh in range(1, heads):
                out = jnp.where(out_lane_head == hh, accs[r][hh * Q_BLOCK:(hh + 1) * Q_BLOCK], out)
            outs.append(out)
        t0 = pl.multiple_of(gi * Q_INFLIGHT * Q_BLOCK, Q_INFLIGHT * Q_BLOCK)
        o_ref[0, pl.ds(t0, Q_INFLIGHT * Q_BLOCK), :] = jnp.concatenate(outs, axis=0).astype(o_ref.dtype)

    def walk_older_keys(gi):
        qs, w_diag, carries, accs = diagonal_tiles(gi)

        def cond(st):
            ks, carries, _ = st
            return jnp.min(open_carry(ks, carries)) < STICK_EXIT

        def body(st):
            ks, carries, accs = st
            w0s = [pl.multiple_of(jnp.maximum(k - K_TILE, 0), Q_BLOCK) for k in ks]
            carries, accs = tiles(qs, w0s, key_off, [k - w for k, w in zip(ks, w0s)], carries, accs)
            return w0s, carries, accs

        _, _, accs = lax.while_loop(cond, body, (w_diag, carries, accs))
        write_out(gi, accs)

    n_groups = seq // (Q_BLOCK * Q_INFLIGHT)

    def q_group(gi, prev_open):
        prev_unfinished = jnp.min(prev_open) < STICK_EXIT
        _, w_diag, carries, accs = diagonal_tiles(gi)
        write_out(gi, accs)

        @pl.when(prev_unfinished)
        def _():
            walk_older_keys(jnp.maximum(gi - 1, 0))

        return open_carry(w_diag, carries)

    last_open = lax.fori_loop(0, n_groups, q_group, jnp.full((rows, 1), STICK_EXIT, F32))

    @pl.when(jnp.min(last_open) < STICK_EXIT)
    def _():
        walk_older_keys(n_groups - 1)


def _stick(q, k, v):
    bn, s, d = q.shape
    blk = pl.BlockSpec((1, s, LANES), lambda i, j: (i, 0, j))
    return pl.pallas_call(
        _stick_kernel,
        grid=(bn, d // LANES),
        in_specs=[blk, blk, blk],
        out_specs=blk,
        out_shape=jax.ShapeDtypeStruct((bn, s, d), BF16),
        compiler_params=_params(2),
        name="stick_breaking",
    )(q, k, v)


def _proj_ln_kernel(y_ref, x_ref, w_ref, g_ref, b_ref, o_ref):
    mix = _mm(y_ref[...], w_ref[...])
    o_ref[...] = _ln_rows(DEEPNORM_ALPHA * x_ref[...] + mix, g_ref[...], b_ref[...])


def _proj_ln(y2, x2, w, g, b):
    n, d = x2.shape
    tile = pl.BlockSpec((TM_PROJ, d), lambda i: (i, 0))
    return pl.pallas_call(
        _proj_ln_kernel,
        grid=(n // TM_PROJ,),
        in_specs=[tile, tile, _resident(w.shape), _resident((1, d)), _resident((1, d))],
        out_specs=tile,
        out_shape=jax.ShapeDtypeStruct((n, d), F32),
        compiler_params=_params(1),
        name="out_proj_ln",
    )(y2, x2, w, g, b)


def kernel(x, mem, w_in_ab, gmlp_ln_g, gmlp_ln_b, gmlp_w_s, gmlp_b_s, conv_w, conv_b, conv_gn_g, conv_gn_b, w_out_ab, w_qkv_c, w_out_c, mem_wq, mem_wk, mem_wv, mem_wo, ffn_w1, ffn_w3, ffn_w2, ln_g, ln_b):
    bn, s, d = x.shape
    n = bn * s
    row = lambda a: a.reshape(1, -1).astype(F32)
    bf = lambda a: a.astype(BF16)
    for layer in range(DEPTH):
        if layer % 2 == 0:
            e = layer // 2
            bs_full = jnp.repeat(gmlp_b_s[e].T, GROUP_DIM, axis=1)
            x = _even_mixer(x, bf(w_in_ab[e]), row(gmlp_ln_g[e]), row(gmlp_ln_b[e]), gmlp_w_s[e], bs_full,
                            conv_w[e], row(conv_b[e]), row(conv_gn_g[e]), row(conv_gn_b[e]), bf(w_out_ab[e]),
                            row(ln_g[layer, 0]), row(ln_b[layer, 0]))
        else:
            o = layer // 2
            x2 = x.reshape(n, d)
            q, k, v = _qkv(x2, bf(w_qkv_c[o]))
            y = _stick(q.reshape(bn, s, d), k.reshape(bn, s, d), v.reshape(bn, s, d))
            x = _proj_ln(y.reshape(n, d), x2, bf(w_out_c[o]), row(ln_g[layer, 0]), row(ln_b[layer, 0])).reshape(bn, s, d)
        kt, vm = _memkv(mem, bf(mem_wk[layer].T), bf(mem_wv[layer]))
        x = _cross(x, kt, vm, bf(mem_wq[layer]), bf(mem_wo[layer]), row(ln_g[layer, 1]), row(ln_b[layer, 1]))
        x = _ffn(x.reshape(n, d), bf(ffn_w1[layer]), bf(ffn_w3[layer]), bf(ffn_w2[layer]),
                 row(ln_g[layer, 2]), row(ln_b[layer, 2])).reshape(bn, s, d)
    return x
```

```python
import functools

import jax
import jax.numpy as jnp
from jax import lax
from jax.experimental import pallas as pl
from jax.experimental.pallas import tpu as pltpu

F32 = jnp.float32
BF16 = jnp.bfloat16

DEPTH = 4
CHUNK = 128
A_GROUPS = 4
B_GROUPS = 4
GROUP_DIM = 128
CONV_WIDTH = 31
C_HEADS = 16
C_HEAD_DIM = 64
MEM_HEADS = 4
DEEPNORM_ALPHA = (2.0 * DEPTH) ** 0.25
LN_EPS = 1e-5

LANES = 128
SUBLANES = 8
CONV_HALO = 32
Q_BLOCK = 64
K_TILE = 256
STICK_EXIT = 88.0
STICK_MASKED = -1e30
SUM_GROUP = 4
Q_INFLIGHT = 32

TM_FFN = 512
TM_CROSS = 1024
TM_EVEN = 1024
TM_PROJ = 1024
VMEM_LIMIT = 56 * 1024 * 1024


def _resident(shape):
    nd = len(shape)
    return pl.BlockSpec(shape, lambda *_: (0,) * nd, pipeline_mode=pl.Buffered(1))


def _params(n_axes):
    return pltpu.CompilerParams(dimension_semantics=("arbitrary",) * n_axes,
                                vmem_limit_bytes=VMEM_LIMIT)


def _ln_rows(y, g, b):
    mu = jnp.mean(y, axis=-1, keepdims=True)
    yc = y - mu
    var = jnp.mean(yc * yc, axis=-1, keepdims=True)
    return yc * lax.rsqrt(var + LN_EPS) * g + b


def _gelu_exact(x):
    return 0.5 * x * (1.0 + lax.erf(x * (0.5 ** 0.5)))


def _silu(x):
    return x * jax.nn.sigmoid(x)


def _mm(a, b):
    return jnp.dot(a, b, preferred_element_type=F32)


def _ffn_kernel(x_ref, w1_ref, w3_ref, w2_ref, g_ref, b_ref, o_ref):
    x = x_ref[...]
    xb = x.astype(BF16)
    h1 = _mm(xb, w1_ref[...])
    h3 = _mm(xb, w3_ref[...])
    act = (_silu(h1) * h3).astype(BF16)
    y = _mm(act, w2_ref[...])
    o_ref[...] = _ln_rows(DEEPNORM_ALPHA * x + y, g_ref[...], b_ref[...])


def _ffn(x2, w1, w3, w2, g, b):
    n, d = x2.shape
    dff = w1.shape[1]
    tile = pl.BlockSpec((TM_FFN, d), lambda i: (i, 0))
    return pl.pallas_call(
        _ffn_kernel,
        grid=(n // TM_FFN,),
        in_specs=[tile, _resident((d, dff)), _resident((d, dff)), _resident((dff, d)),
                  _resident((1, d)), _resident((1, d))],
        out_specs=tile,
        out_shape=jax.ShapeDtypeStruct((n, d), F32),
        compiler_params=_params(1),
        name="ffn_ln",
    )(x2, w1, w3, w2, g, b)


def _memkv_kernel(mem_ref, wkt_ref, wv_ref, kt_ref, v_ref):
    mb = mem_ref[0].astype(BF16)
    kt = lax.dot_general(wkt_ref[...], mb, (((1,), (1,)), ((), ())), preferred_element_type=F32)
    kt_ref[0] = kt.astype(BF16)
    v_ref[0] = _mm(mb, wv_ref[...]).astype(BF16)


def _memkv(mem, wkt, wv):
    bn, m, d = mem.shape
    return pl.pallas_call(
        _memkv_kernel,
        grid=(bn,),
        in_specs=[pl.BlockSpec((1, m, d), lambda i: (i, 0, 0)), _resident((d, d)), _resident((d, d))],
        out_specs=[pl.BlockSpec((1, d, m), lambda i: (i, 0, 0)), pl.BlockSpec((1, m, d), lambda i: (i, 0, 0))],
        out_shape=[jax.ShapeDtypeStruct((bn, d, m), BF16), jax.ShapeDtypeStruct((bn, m, d), BF16)],
        compiler_params=_params(1),
        name="mem_kv",
    )(mem, wkt, wv)


def _cross_kernel(x_ref, kt_ref, v_ref, wq_ref, wo_ref, g_ref, b_ref, o_ref):
    x = x_ref[0]
    d = x.shape[-1]
    hd = d // MEM_HEADS
    q = (_mm(x.astype(BF16), wq_ref[...]) * (hd ** -0.5)).astype(BF16)
    heads = [slice(h * hd, (h + 1) * hd) for h in range(MEM_HEADS)]
    scores = [_mm(q[:, sl], kt_ref[0, sl, :]) for sl in heads]
    outs = []
    for s, sl in zip(scores, heads):
        e = jnp.exp(s - jnp.max(s, axis=-1, keepdims=True))
        p = (e / jnp.sum(e, axis=-1, keepdims=True)).astype(BF16)
        outs.append(_mm(p, v_ref[0, :, sl]))
    o = jnp.concatenate(outs, axis=-1).astype(BF16)
    cross = _mm(o, wo_ref[...])
    o_ref[0] = _ln_rows(DEEPNORM_ALPHA * x + cross, g_ref[...], b_ref[...])


def _cross(x, kt, v, wq, wo, g, b):
    bn, s, d = x.shape
    m = v.shape[1]
    tile = pl.BlockSpec((1, TM_CROSS, d), lambda i, j: (i, j, 0))
    return pl.pallas_call(
        _cross_kernel,
        grid=(bn, s // TM_CROSS),
        in_specs=[tile,
                  pl.BlockSpec((1, d, m), lambda i, j: (i, 0, 0)),
                  pl.BlockSpec((1, m, d), lambda i, j: (i, 0, 0)),
                  _resident((d, d)), _resident((d, d)), _resident((1, d)), _resident((1, d))],
        out_specs=tile,
        out_shape=jax.ShapeDtypeStruct((bn, s, d), F32),
        compiler_params=_params(2),
        name="cross_ln",
    )(x, kt, v, wq, wo, g, b)


def _even_kernel(x_ref, win_ref, lng_ref, lnb_ref, ws_ref, bs_ref, cw_ref, cb_ref, gng_ref, gnb_ref,
                 wout_ref, g_ref, b_ref, o_ref, hbuf_ref, shift_ref):
    tm = x_ref.shape[1]
    aw = A_GROUPS * GROUP_DIM
    bw = B_GROUPS * GROUP_DIM
    x = x_ref[0]
    h = _mm(x.astype(BF16), win_ref[...])
    u = _gelu_exact(h[:, 0:aw])
    v = _gelu_exact(h[:, aw:2 * aw])
    a = h[:, 2 * aw:2 * aw + bw]
    gate = h[:, 2 * aw + bw:]

    row = lax.broadcasted_iota(jnp.int32, (CHUNK, CHUNK), 0)
    col = lax.broadcasted_iota(jnp.int32, (CHUNK, CHUNK), 1)
    causal = row >= col
    y_parts = []
    for g in range(A_GROUPS):
        gs = slice(g * GROUP_DIM, (g + 1) * GROUP_DIM)
        vn = _ln_rows(v[:, gs], lng_ref[:, gs], lnb_ref[:, gs]).astype(BF16)
        wg = jnp.where(causal, ws_ref[g], 0.0).astype(BF16)
        mixed = [_mm(wg, vn[c * CHUNK:(c + 1) * CHUNK, :]) + bs_ref[:, gs]
                 for c in range(tm // CHUNK)]
        y_parts.append(u[:, gs] * jnp.concatenate(mixed, axis=0))

    @pl.when(pl.program_id(1) == 0)
    def _():
        hbuf_ref[0:CONV_HALO, :] = jnp.zeros((CONV_HALO, bw), F32)

    hbuf_ref[CONV_HALO:CONV_HALO + tm, :] = a * jax.nn.sigmoid(gate)
    first = CONV_HALO - (CONV_WIDTH - 1)
    span = tm + CONV_HALO - SUBLANES
    for s in range(1, SUBLANES):
        shift_ref[s - 1, 0:span, :] = hbuf_ref[s:s + span, :]
    strip = 32
    conv_rows = []
    for r0 in range(0, tm, strip):
        acc = jnp.broadcast_to(cb_ref[...], (strip, bw))
        for j in range(CONV_WIDTH):
            s = (first + j) % SUBLANES
            a0 = r0 + (first + j) // SUBLANES * SUBLANES
            src = hbuf_ref[a0:a0 + strip, :] if s == 0 else shift_ref[s - 1, a0:a0 + strip, :]
            acc = acc + cw_ref[j:j + 1, :] * src
        conv_rows.append(acc)
    conv = jnp.concatenate(conv_rows, axis=0)
    hbuf_ref[0:CONV_HALO, :] = hbuf_ref[tm:tm + CONV_HALO, :]
    for g in range(B_GROUPS):
        gs = slice(g * GROUP_DIM, (g + 1) * GROUP_DIM)
        y_parts.append(_silu(_ln_rows(conv[:, gs], gng_ref[:, gs], gnb_ref[:, gs])))

    ycat = jnp.concatenate(y_parts, axis=-1).astype(BF16)
    mix = _mm(ycat, wout_ref[...])
    o_ref[0] = _ln_rows(DEEPNORM_ALPHA * x + mix, g_ref[...], b_ref[...])


def _even_mixer(x, win, lng, lnb, ws, bs_full, cw, cb, gng, gnb, wout, g, b):
    bn, s, d = x.shape
    aw = A_GROUPS * GROUP_DIM
    bw = B_GROUPS * GROUP_DIM
    tile = pl.BlockSpec((1, TM_EVEN, d), lambda i, j: (i, j, 0))
    return pl.pallas_call(
        _even_kernel,
        grid=(bn, s // TM_EVEN),
        in_specs=[tile, _resident(win.shape), _resident((1, aw)), _resident((1, aw)),
                  _resident(ws.shape), _resident(bs_full.shape), _resident(cw.shape), _resident((1, bw)),
                  _resident((1, bw)), _resident((1, bw)), _resident(wout.shape),
                  _resident((1, d)), _resident((1, d))],
        out_specs=tile,
        out_shape=jax.ShapeDtypeStruct((bn, s, d), F32),
        scratch_shapes=[pltpu.VMEM((TM_EVEN + CONV_HALO, bw), F32),
                        pltpu.VMEM((SUBLANES - 1, TM_EVEN + CONV_HALO - SUBLANES, bw), F32)],
        compiler_params=_params(2),
        name="even_mixer_ln",
    )(x, win, lng, lnb, ws, bs_full, cw, cb, gng, gnb, wout, g, b)


def _qkv_kernel(x_ref, w_ref, q_ref, k_ref, v_ref):
    d = x_ref.shape[-1]
    qkv = _mm(x_ref[...].astype(BF16), w_ref[...])
    q_ref[...] = (qkv[:, 0:d] * (C_HEAD_DIM ** -0.5)).astype(BF16)
    k_ref[...] = qkv[:, d:2 * d].astype(BF16)
    v_ref[...] = qkv[:, 2 * d:3 * d].astype(BF16)


def _qkv(x2, w):
    n, d = x2.shape
    tile = pl.BlockSpec((TM_PROJ, d), lambda i: (i, 0))
    out = jax.ShapeDtypeStruct((n, d), BF16)
    return pl.pallas_call(
        _qkv_kernel,
        grid=(n // TM_PROJ,),
        in_specs=[tile, _resident(w.shape)],
        out_specs=[tile, tile, tile],
        out_shape=[out, out, out],
        compiler_params=_params(1),
        name="qkv_proj",
    )(x2, w)


def _stick_kernel(q_ref, k_ref, v_ref, o_ref):
    seq = q_ref.shape[1]
    heads = LANES // C_HEAD_DIM
    rows = heads * Q_BLOCK
    rj = lax.broadcasted_iota(jnp.int32, (K_TILE, K_TILE), 0)
    cs = lax.broadcasted_iota(jnp.int32, (K_TILE, K_TILE), 1)
    from_key_on = (rj >= cs).astype(BF16)
    lane_head = lax.broadcasted_iota(jnp.int32, (rows, LANES), 1) // C_HEAD_DIM
    row_head = lax.broadcasted_iota(jnp.int32, (rows, LANES), 0) // Q_BLOCK
    in_head = lane_head == row_head
    out_lane_head = lax.broadcasted_iota(jnp.int32, (Q_BLOCK, LANES), 1) // C_HEAD_DIM
    key_off = lax.broadcasted_iota(jnp.int32, (rows, K_TILE), 1)
    key_minus_row = key_off - lax.broadcasted_iota(jnp.int32, (rows, K_TILE), 0) % Q_BLOCK

    def tiles(qs, w0s, rel, bounds, carries, accs):
        n = len(qs)
        zs = []
        for q_r, w0 in zip(qs, w0s):
            kt = k_ref[0, pl.ds(w0, K_TILE), :]
            zs.append(lax.dot_general(q_r, kt, (((1,), (1,)), ((), ())), preferred_element_type=F32))
        sps, sums = [], []
        for g0 in range(0, n, SUM_GROUP):
            his, los = [], []
            for r in range(g0, g0 + SUM_GROUP):
                if rel is key_minus_row and r * Q_BLOCK >= K_TILE - Q_BLOCK:
                    tail_ok = rel[:, K_TILE - LANES:] < K_TILE - Q_BLOCK
                    z = jnp.concatenate([zs[r][:, :K_TILE - LANES],
                                         jnp.where(tail_ok, zs[r][:, K_TILE - LANES:], STICK_MASKED)], axis=1)
                else:
                    z = jnp.where(rel < bounds[r], zs[r], STICK_MASKED)
                zs[r] = z
                neg_abs = pltpu.bitcast(pltpu.bitcast(z, jnp.int32) | jnp.int32(-2 ** 31), F32)
                sp = jnp.maximum(z, 0.0) + jnp.log(1.0 + jnp.exp(neg_abs))
                hi = sp.astype(BF16)
                sps.append(sp)
                his.append(hi)
                los.append((sp - hi.astype(F32)).astype(BF16))
            sums.append(_mm(jnp.concatenate(his + los, axis=0), from_key_on))
        new_c, new_a = [], []
        for r in range(n):
            s2 = sums[r // SUM_GROUP]
            i = r % SUM_GROUP
            tail = s2[i * rows:(i + 1) * rows] + s2[(SUM_GROUP + i) * rows:(SUM_GROUP + i + 1) * rows]
            logw = zs[r] - tail
            if carries is not None:
                logw = logw - carries[r]
            vt = v_ref[0, pl.ds(w0s[r], K_TILE), :]
            pv = _mm(jnp.exp(logw).astype(BF16), vt)
            rowsum = jnp.sum(sps[r], axis=1, keepdims=True)
            new_a.append(pv if accs is None else accs[r] + pv)
            new_c.append(rowsum if carries is None else carries[r] + rowsum)
        return new_c, new_a

    def diagonal_tiles(gi):
        qs, w_diag, bounds = [], [], []
        for r in range(Q_INFLIGHT):
            t0 = pl.multiple_of((gi * Q_INFLIGHT + r) * Q_BLOCK, Q_BLOCK)
            q2 = q_ref[0, pl.ds(t0, Q_BLOCK), :]
            q2 = jnp.concatenate([q2] * heads, axis=0)
            qs.append(jnp.where(in_head, q2, jnp.zeros_like(q2)))
            w0 = pl.multiple_of(jnp.maximum(t0 + Q_BLOCK - K_TILE, 0), Q_BLOCK)
            w_diag.append(w0)
            bounds.append(t0 - w0)
        carries, accs = tiles(qs, w_diag, key_minus_row, bounds, None, None)
        return qs, w_diag, carries, accs

    def open_carry(ks, carries):
        return functools.reduce(jnp.minimum, [jnp.where(k > 0, c, STICK_EXIT) for k, c in zip(ks, carries)])

    def write_out(gi, accs):
        outs = []
        for r in range(Q_INFLIGHT):
            out = accs[r][0:Q_BLOCK]
            for hh in range(1, heads):
                out = jnp.where(out_lane_head == hh, accs[r][hh * Q_BLOCK:(hh + 1) * Q_BLOCK], out)
            outs.append(out)
        t0 = pl.multiple_of(gi * Q_INFLIGHT * Q_BLOCK, Q_INFLIGHT * Q_BLOCK)
        o_ref[0, pl.ds(t0, Q_INFLIGHT * Q_BLOCK), :] = jnp.concatenate(outs, axis=0).astype(o_ref.dtype)

    def walk_older_keys(gi):
        qs, w_diag, carries, accs = diagonal_tiles(gi)

        def cond(st):
            ks, carries, _ = st
            return jnp.min(open_carry(ks, carries)) < STICK_EXIT

        def body(st):
            ks, carries, accs = st
            w0s = [pl.multiple_of(jnp.maximum(k - K_TILE, 0), Q_BLOCK) for k in ks]
            carries, accs = tiles(qs, w0s, key_off, [k - w for k, w in zip(ks, w0s)], carries, accs)
            return w0s, carries, accs

        _, _, accs = lax.while_loop(cond, body, (w_diag, carries, accs))
        write_out(gi, accs)

    n_groups = seq // (Q_BLOCK * Q_INFLIGHT)

    def q_group(gi, prev_open):
        prev_unfinished = jnp.min(prev_open) < STICK_EXIT
        _, w_diag, carries, accs = diagonal_tiles(gi)
        write_out(gi, accs)

        @pl.when(prev_unfinished)
        def _():
            walk_older_keys(jnp.maximum(gi - 1, 0))

        return open_carry(w_diag, carries)

    last_open = lax.fori_loop(0, n_groups, q_group, jnp.full((rows, 1), STICK_EXIT, F32))

    @pl.when(jnp.min(last_open) < STICK_EXIT)
    def _():
        walk_older_keys(n_groups - 1)


def _stick(q, k, v):
    bn, s, d = q.shape
    blk = pl.BlockSpec((1, s, LANES), lambda i, j: (i, 0, j))
    return pl.pallas_call(
        _stick_kernel,
        grid=(bn, d // LANES),
        in_specs=[blk, blk, blk],
        out_specs=blk,
        out_shape=jax.ShapeDtypeStruct((bn, s, d), BF16),
        compiler_params=_params(2),
        name="stick_breaking",
    )(q, k, v)


def _proj_ln_kernel(y_ref, x_ref, w_ref, g_ref, b_ref, o_ref):
    mix = _mm(y_ref[...], w_ref[...])
    o_ref[...] = _ln_rows(DEEPNORM_ALPHA * x_ref[...] + mix, g_ref[...], b_ref[...])


def _proj_ln(y2, x2, w, g, b):
    n, d = x2.shape
    tile = pl.BlockSpec((TM_PROJ, d), lambda i: (i, 0))
    return pl.pallas_call(
        _proj_ln_kernel,
        grid=(n // TM_PROJ,),
        in_specs=[tile, tile, _resident(w.shape), _resident((1, d)), _resident((1, d))],
        out_specs=tile,
        out_shape=jax.ShapeDtypeStruct((n, d), F32),
        compiler_params=_params(1),
        name="out_proj_ln",
    )(y2, x2, w, g, b)


def kernel(x, mem, w_in_ab, gmlp_ln_g, gmlp_ln_b, gmlp_w_s, gmlp_b_s, conv_w, conv_b, conv_gn_g, conv_gn_b, w_out_ab, w_qkv_c, w_out_c, mem_wq, mem_wk, mem_wv, mem_wo, ffn_w1, ffn_w3, ffn_w2, ln_g, ln_b):
    bn, s, d = x.shape
    n = bn * s
    row = lambda a: a.reshape(1, -1).astype(F32)
    bf = lambda a: a.astype(BF16)
    for layer in range(DEPTH):
        if layer % 2 == 0:
            e = layer // 2
            bs_full = jnp.repeat(gmlp_b_s[e].T, GROUP_DIM, axis=1)
            x = _even_mixer(x, bf(w_in_ab[e]), row(gmlp_ln_g[e]), row(gmlp_ln_b[e]), gmlp_w_s[e], bs_full,
                            conv_w[e], row(conv_b[e]), row(conv_gn_g[e]), row(conv_gn_b[e]), bf(w_out_ab[e]),
                            row(ln_g[layer, 0]), row(ln_b[layer, 0]))
        else:
            o = layer // 2
            x2 = x.reshape(n, d)
            q, k, v = _qkv(x2, bf(w_qkv_c[o]))
            y = _stick(q.reshape(bn, s, d), k.reshape(bn, s, d), v.reshape(bn, s, d))
            x = _proj_ln(y.reshape(n, d), x2, bf(w_out_c[o]), row(ln_g[layer, 0]), row(ln_b[layer, 0])).reshape(bn, s, d)
        kt, vm = _memkv(mem, bf(mem_wk[layer].T), bf(mem_wv[layer]))
        x = _cross(x, kt, vm, bf(mem_wq[layer]), bf(mem_wo[layer]), row(ln_g[layer, 1]), row(ln_b[layer, 1]))
        x = _ffn(x.reshape(n, d), bf(ffn_w1[layer]), bf(ffn_w3[layer]), bf(ffn_w2[layer]),
                 row(ln_g[layer, 2]), row(ln_b[layer, 2])).reshape(bn, s, d)
    return x
```

```python
import functools

import jax
import jax.numpy as jnp
from jax import lax
from jax.experimental import pallas as pl
from jax.experimental.pallas import tpu as pltpu

F32 = jnp.float32
BF16 = jnp.bfloat16

DEPTH = 4
CHUNK = 128
A_GROUPS = 4
B_GROUPS = 4
GROUP_DIM = 128
CONV_WIDTH = 31
C_HEADS = 16
C_HEAD_DIM = 64
MEM_HEADS = 4
DEEPNORM_ALPHA = (2.0 * DEPTH) ** 0.25
LN_EPS = 1e-5

LANES = 128
SUBLANES = 8
CONV_HALO = 32
Q_BLOCK = 64
K_TILE = 256
STICK_EXIT = 88.0
STICK_MASKED = -1e30
SUM_GROUP = 8
Q_INFLIGHT = 32

TM_FFN = 512
TM_CROSS = 1024
TM_EVEN = 1024
TM_PROJ = 1024
VMEM_LIMIT = 56 * 1024 * 1024


def _resident(shape):
    nd = len(shape)
    return pl.BlockSpec(shape, lambda *_: (0,) * nd, pipeline_mode=pl.Buffered(1))


def _params(n_axes):
    return pltpu.CompilerParams(dimension_semantics=("arbitrary",) * n_axes,
                                vmem_limit_bytes=VMEM_LIMIT)


def _ln_rows(y, g, b):
    mu = jnp.mean(y, axis=-1, keepdims=True)
    yc = y - mu
    var = jnp.mean(yc * yc, axis=-1, keepdims=True)
    return yc * lax.rsqrt(var + LN_EPS) * g + b


def _gelu_exact(x):
    return 0.5 * x * (1.0 + lax.erf(x * (0.5 ** 0.5)))


def _silu(x):
    return x * jax.nn.sigmoid(x)


def _mm(a, b):
    return jnp.dot(a, b, preferred_element_type=F32)


def _ffn_kernel(x_ref, w1_ref, w3_ref, w2_ref, g_ref, b_ref, o_ref):
    x = x_ref[...]
    xb = x.astype(BF16)
    h1 = _mm(xb, w1_ref[...])
    h3 = _mm(xb, w3_ref[...])
    act = (_silu(h1) * h3).astype(BF16)
    y = _mm(act, w2_ref[...])
    o_ref[...] = _ln_rows(DEEPNORM_ALPHA * x + y, g_ref[...], b_ref[...])


def _ffn(x2, w1, w3, w2, g, b):
    n, d = x2.shape
    dff = w1.shape[1]
    assert n % TM_FFN == 0 and w1.shape == w3.shape == (d, dff) and w2.shape == (dff, d)
    tile = pl.BlockSpec((TM_FFN, d), lambda i: (i, 0))
    return pl.pallas_call(
        _ffn_kernel,
        grid=(n // TM_FFN,),
        in_specs=[tile, _resident((d, dff)), _resident((d, dff)), _resident((dff, d)),
                  _resident((1, d)), _resident((1, d))],
        out_specs=tile,
        out_shape=jax.ShapeDtypeStruct((n, d), F32),
        compiler_params=_params(1),
        name="ffn_ln",
    )(x2, w1, w3, w2, g, b)


def _memkv_kernel(mem_ref, wkt_ref, wv_ref, kt_ref, v_ref):
    mb = mem_ref[0].astype(BF16)
    kt = lax.dot_general(wkt_ref[...], mb, (((1,), (1,)), ((), ())), preferred_element_type=F32)
    kt_ref[0] = kt.astype(BF16)
    v_ref[0] = _mm(mb, wv_ref[...]).astype(BF16)


def _memkv(mem, wkt, wv):
    bn, m, d = mem.shape
    return pl.pallas_call(
        _memkv_kernel,
        grid=(bn,),
        in_specs=[pl.BlockSpec((1, m, d), lambda i: (i, 0, 0)), _resident((d, d)), _resident((d, d))],
        out_specs=[pl.BlockSpec((1, d, m), lambda i: (i, 0, 0)), pl.BlockSpec((1, m, d), lambda i: (i, 0, 0))],
        out_shape=[jax.ShapeDtypeStruct((bn, d, m), BF16), jax.ShapeDtypeStruct((bn, m, d), BF16)],
        compiler_params=_params(1),
        name="mem_kv",
    )(mem, wkt, wv)


def _cross_kernel(x_ref, kt_ref, v_ref, wq_ref, wo_ref, g_ref, b_ref, o_ref):
    x = x_ref[0]
    d = x.shape[-1]
    hd = d // MEM_HEADS
    q = (_mm(x.astype(BF16), wq_ref[...]) * (hd ** -0.5)).astype(BF16)
    heads = [slice(h * hd, (h + 1) * hd) for h in range(MEM_HEADS)]
    scores = [_mm(q[:, sl], kt_ref[0, sl, :]) for sl in heads]
    outs = []
    for s, sl in zip(scores, heads):
        e = jnp.exp(s - jnp.max(s, axis=-1, keepdims=True))
        p = (e / jnp.sum(e, axis=-1, keepdims=True)).astype(BF16)
        outs.append(_mm(p, v_ref[0, :, sl]))
    o = jnp.concatenate(outs, axis=-1).astype(BF16)
    cross = _mm(o, wo_ref[...])
    o_ref[0] = _ln_rows(DEEPNORM_ALPHA * x + cross, g_ref[...], b_ref[...])


def _cross(x, kt, v, wq, wo, g, b):
    bn, s, d = x.shape
    m = v.shape[1]
    assert s % TM_CROSS == 0 and d % MEM_HEADS == 0 and kt.shape == (bn, d, m) and v.shape == (bn, m, d)
    tile = pl.BlockSpec((1, TM_CROSS, d), lambda i, j: (i, j, 0))
    return pl.pallas_call(
        _cross_kernel,
        grid=(bn, s // TM_CROSS),
        in_specs=[tile,
                  pl.BlockSpec((1, d, m), lambda i, j: (i, 0, 0)),
                  pl.BlockSpec((1, m, d), lambda i, j: (i, 0, 0)),
                  _resident((d, d)), _resident((d, d)), _resident((1, d)), _resident((1, d))],
        out_specs=tile,
        out_shape=jax.ShapeDtypeStruct((bn, s, d), F32),
        compiler_params=_params(2),
        name="cross_ln",
    )(x, kt, v, wq, wo, g, b)


def _even_kernel(x_ref, win_ref, lng_ref, lnb_ref, ws_ref, bs_ref, cw_ref, cb_ref, gng_ref, gnb_ref,
                 wout_ref, g_ref, b_ref, o_ref, hbuf_ref, shift_ref):
    tm = x_ref.shape[1]
    aw = A_GROUPS * GROUP_DIM
    bw = B_GROUPS * GROUP_DIM
    x = x_ref[0]
    h = _mm(x.astype(BF16), win_ref[...])
    u = _gelu_exact(h[:, 0:aw])
    v = _gelu_exact(h[:, aw:2 * aw])
    a = h[:, 2 * aw:2 * aw + bw]
    gate = h[:, 2 * aw + bw:]

    row = lax.broadcasted_iota(jnp.int32, (CHUNK, CHUNK), 0)
    col = lax.broadcasted_iota(jnp.int32, (CHUNK, CHUNK), 1)
    causal = row >= col
    y_parts = []
    for g in range(A_GROUPS):
        gs = slice(g * GROUP_DIM, (g + 1) * GROUP_DIM)
        vn = _ln_rows(v[:, gs], lng_ref[:, gs], lnb_ref[:, gs]).astype(BF16)
        wg = jnp.where(causal, ws_ref[g], 0.0).astype(BF16)
        mixed = [_mm(wg, vn[c * CHUNK:(c + 1) * CHUNK, :]) + bs_ref[:, gs]
                 for c in range(tm // CHUNK)]
        y_parts.append(u[:, gs] * jnp.concatenate(mixed, axis=0))

    @pl.when(pl.program_id(1) == 0)
    def _():
        hbuf_ref[0:CONV_HALO, :] = jnp.zeros((CONV_HALO, bw), F32)

    hbuf_ref[CONV_HALO:CONV_HALO + tm, :] = a * jax.nn.sigmoid(gate)
    first = CONV_HALO - (CONV_WIDTH - 1)
    span = tm + CONV_HALO - SUBLANES
    for s in range(1, SUBLANES):
        shift_ref[s - 1, 0:span, :] = hbuf_ref[s:s + span, :]
    strip = 32
    conv_rows = []
    for r0 in range(0, tm, strip):
        acc = jnp.broadcast_to(cb_ref[...], (strip, bw))
        for j in range(CONV_WIDTH):
            s = (first + j) % SUBLANES
            a0 = r0 + (first + j) // SUBLANES * SUBLANES
            src = hbuf_ref[a0:a0 + strip, :] if s == 0 else shift_ref[s - 1, a0:a0 + strip, :]
            acc = acc + cw_ref[j:j + 1, :] * src
        conv_rows.append(acc)
    conv = jnp.concatenate(conv_rows, axis=0)
    hbuf_ref[0:CONV_HALO, :] = hbuf_ref[tm:tm + CONV_HALO, :]
    for g in range(B_GROUPS):
        gs = slice(g * GROUP_DIM, (g + 1) * GROUP_DIM)
        y_parts.append(_silu(_ln_rows(conv[:, gs], gng_ref[:, gs], gnb_ref[:, gs])))

    ycat = jnp.concatenate(y_parts, axis=-1).astype(BF16)
    mix = _mm(ycat, wout_ref[...])
    o_ref[0] = _ln_rows(DEEPNORM_ALPHA * x + mix, g_ref[...], b_ref[...])


def _even_mixer(x, win, lng, lnb, ws, bs_full, cw, cb, gng, gnb, wout, g, b):
    bn, s, d = x.shape
    aw = A_GROUPS * GROUP_DIM
    bw = B_GROUPS * GROUP_DIM
    assert s % TM_EVEN == 0 and TM_EVEN % CHUNK == 0 and CONV_WIDTH - 1 <= CONV_HALO
    assert win.shape == (d, 2 * aw + 2 * bw) and wout.shape == (aw + bw, d) and cw.shape == (CONV_WIDTH, bw)
    tile = pl.BlockSpec((1, TM_EVEN, d), lambda i, j: (i, j, 0))
    return pl.pallas_call(
        _even_kernel,
        grid=(bn, s // TM_EVEN),
        in_specs=[tile, _resident(win.shape), _resident((1, aw)), _resident((1, aw)),
                  _resident(ws.shape), _resident(bs_full.shape), _resident(cw.shape), _resident((1, bw)),
                  _resident((1, bw)), _resident((1, bw)), _resident(wout.shape),
                  _resident((1, d)), _resident((1, d))],
        out_specs=tile,
        out_shape=jax.ShapeDtypeStruct((bn, s, d), F32),
        scratch_shapes=[pltpu.VMEM((TM_EVEN + CONV_HALO, bw), F32),
                        pltpu.VMEM((SUBLANES - 1, TM_EVEN + CONV_HALO - SUBLANES, bw), F32)],
        compiler_params=_params(2),
        name="even_mixer_ln",
    )(x, win, lng, lnb, ws, bs_full, cw, cb, gng, gnb, wout, g, b)


def _qkv_kernel(x_ref, w_ref, q_ref, k_ref, v_ref):
    d = x_ref.shape[-1]
    qkv = _mm(x_ref[...].astype(BF16), w_ref[...])
    q_ref[...] = (qkv[:, 0:d] * (C_HEAD_DIM ** -0.5)).astype(BF16)
    k_ref[...] = qkv[:, d:2 * d].astype(BF16)
    v_ref[...] = qkv[:, 2 * d:3 * d].astype(BF16)


def _qkv(x2, w):
    n, d = x2.shape
    assert n % TM_PROJ == 0 and w.shape == (d, 3 * d)
    tile = pl.BlockSpec((TM_PROJ, d), lambda i: (i, 0))
    out = jax.ShapeDtypeStruct((n, d), BF16)
    return pl.pallas_call(
        _qkv_kernel,
        grid=(n // TM_PROJ,),
        in_specs=[tile, _resident(w.shape)],
        out_specs=[tile, tile, tile],
        out_shape=[out, out, out],
        compiler_params=_params(1),
        name="qkv_proj",
    )(x2, w)


def _stick_kernel(q_ref, k_ref, v_ref, o_ref):
    seq = q_ref.shape[1]
    heads = LANES // C_HEAD_DIM
    rows = heads * Q_BLOCK
    rj = lax.broadcasted_iota(jnp.int32, (K_TILE, K_TILE), 0)
    cs = lax.broadcasted_iota(jnp.int32, (K_TILE, K_TILE), 1)
    from_key_on = (rj >= cs).astype(BF16)
    lane_head = lax.broadcasted_iota(jnp.int32, (rows, LANES), 1) // C_HEAD_DIM
    row_head = lax.broadcasted_iota(jnp.int32, (rows, LANES), 0) // Q_BLOCK
    in_head = lane_head == row_head
    out_lane_head = lax.broadcasted_iota(jnp.int32, (Q_BLOCK, LANES), 1) // C_HEAD_DIM
    key_off = lax.broadcasted_iota(jnp.int32, (rows, K_TILE), 1)
    key_minus_row = key_off - lax.broadcasted_iota(jnp.int32, (rows, K_TILE), 0) % Q_BLOCK

    def tiles(qs, w0s, rel, bounds, carries, accs):
        n = len(qs)
        zs = []
        for q_r, w0 in zip(qs, w0s):
            kt = k_ref[0, pl.ds(w0, K_TILE), :]
            zs.append(lax.dot_general(q_r, kt, (((1,), (1,)), ((), ())), preferred_element_type=F32))
        sps, sums = [], []
        for g0 in range(0, n, SUM_GROUP):
            his, los = [], []
            for r in range(g0, g0 + SUM_GROUP):
                if rel is key_minus_row and r * Q_BLOCK >= K_TILE - Q_BLOCK:
                    tail_ok = rel[:, K_TILE - LANES:] < K_TILE - Q_BLOCK
                    z = jnp.concatenate([zs[r][:, :K_TILE - LANES],
                                         jnp.where(tail_ok, zs[r][:, K_TILE - LANES:], STICK_MASKED)], axis=1)
                else:
                    z = jnp.where(rel < bounds[r], zs[r], STICK_MASKED)
                zs[r] = z
                neg_abs = pltpu.bitcast(pltpu.bitcast(z, jnp.int32) | jnp.int32(-2 ** 31), F32)
                sp = jnp.maximum(z, 0.0) + jnp.log(1.0 + jnp.exp(neg_abs))
                hi = sp.astype(BF16)
                sps.append(sp)
                his.append(hi)
                los.append((sp - hi.astype(F32)).astype(BF16))
            sums.append(_mm(jnp.concatenate(his + los, axis=0), from_key_on))
        new_c, new_a = [], []
        for r in range(n):
            s2 = sums[r // SUM_GROUP]
            i = r % SUM_GROUP
            tail = s2[i * rows:(i + 1) * rows] + s2[(SUM_GROUP + i) * rows:(SUM_GROUP + i + 1) * rows]
            logw = zs[r] - tail
            if carries is not None:
                logw = logw - carries[r]
            vt = v_ref[0, pl.ds(w0s[r], K_TILE), :]
            pv = _mm(jnp.exp(logw).astype(BF16), vt)
            rowsum = jnp.sum(sps[r], axis=1, keepdims=True)
            new_a.append(pv if accs is None else accs[r] + pv)
            new_c.append(rowsum if carries is None else carries[r] + rowsum)
        return new_c, new_a

    def diagonal_tiles(gi):
        qs, w_diag, bounds = [], [], []
        for r in range(Q_INFLIGHT):
            t0 = pl.multiple_of((gi * Q_INFLIGHT + r) * Q_BLOCK, Q_BLOCK)
            q2 = q_ref[0, pl.ds(t0, Q_BLOCK), :]
            q2 = jnp.concatenate([q2] * heads, axis=0)
            qs.append(jnp.where(in_head, q2, jnp.zeros_like(q2)))
            w0 = pl.multiple_of(jnp.maximum(t0 + Q_BLOCK - K_TILE, 0), Q_BLOCK)
            w_diag.append(w0)
            bounds.append(t0 - w0)
        carries, accs = tiles(qs, w_diag, key_minus_row, bounds, None, None)
        return qs, w_diag, carries, accs

    def open_carry(ks, carries):
        return functools.reduce(jnp.minimum, [jnp.where(k > 0, c, STICK_EXIT) for k, c in zip(ks, carries)])

    def write_out(gi, accs):
        outs = []
        for r in range(Q_INFLIGHT):
            out = accs[r][0:Q_BLOCK]
            for hh in range(1, heads):
                out = jnp.where(out_lane_head == hh, accs[r][hh * Q_BLOCK:(hh + 1) * Q_BLOCK], out)
            outs.append(out)
        t0 = pl.multiple_of(gi * Q_INFLIGHT * Q_BLOCK, Q_INFLIGHT * Q_BLOCK)
        o_ref[0, pl.ds(t0, Q_INFLIGHT * Q_BLOCK), :] = jnp.concatenate(outs, axis=0).astype(o_ref.dtype)

    def walk_older_keys(gi):
        qs, w_diag, carries, accs = diagonal_tiles(gi)

        def cond(st):
            ks, carries, _ = st
            return jnp.min(open_carry(ks, carries)) < STICK_EXIT

        def body(st):
            ks, carries, accs = st
            w0s = [pl.multiple_of(jnp.maximum(k - K_TILE, 0), Q_BLOCK) for k in ks]
            carries, accs = tiles(qs, w0s, key_off, [k - w for k, w in zip(ks, w0s)], carries, accs)
            return w0s, carries, accs

        _, _, accs = lax.while_loop(cond, body, (w_diag, carries, accs))
        write_out(gi, accs)

    n_groups = seq // (Q_BLOCK * Q_INFLIGHT)

    def q_group(gi, prev_open):
        prev_unfinished = jnp.min(prev_open) < STICK_EXIT
        _, w_diag, carries, accs = diagonal_tiles(gi)
        write_out(gi, accs)

        @pl.when(prev_unfinished)
        def _():
            walk_older_keys(jnp.maximum(gi - 1, 0))

        return open_carry(w_diag, carries)

    last_open = lax.fori_loop(0, n_groups, q_group, jnp.full((rows, 1), STICK_EXIT, F32))

    @pl.when(jnp.min(last_open) < STICK_EXIT)
    def _():
        walk_older_keys(n_groups - 1)


def _stick(q, k, v):
    bn, s, d = q.shape
    assert s % (Q_BLOCK * Q_INFLIGHT) == 0 and s >= K_TILE and d % LANES == 0 and LANES % C_HEAD_DIM == 0
    assert Q_INFLIGHT % SUM_GROUP == 0 and K_TILE % Q_BLOCK == 0 and K_TILE > LANES
    blk = pl.BlockSpec((1, s, LANES), lambda i, j: (i, 0, j))
    return pl.pallas_call(
        _stick_kernel,
        grid=(bn, d // LANES),
        in_specs=[blk, blk, blk],
        out_specs=blk,
        out_shape=jax.ShapeDtypeStruct((bn, s, d), BF16),
        compiler_params=_params(2),
        name="stick_breaking",
    )(q, k, v)


def _proj_ln_kernel(y_ref, x_ref, w_ref, g_ref, b_ref, o_ref):
    mix = _mm(y_ref[...], w_ref[...])
    o_ref[...] = _ln_rows(DEEPNORM_ALPHA * x_ref[...] + mix, g_ref[...], b_ref[...])


def _proj_ln(y2, x2, w, g, b):
    n, d = x2.shape
    assert n % TM_PROJ == 0 and y2.shape == (n, d) and w.shape == (d, d)
    tile = pl.BlockSpec((TM_PROJ, d), lambda i: (i, 0))
    return pl.pallas_call(
        _proj_ln_kernel,
        grid=(n // TM_PROJ,),
        in_specs=[tile, tile, _resident(w.shape), _resident((1, d)), _resident((1, d))],
        out_specs=tile,
        out_shape=jax.ShapeDtypeStruct((n, d), F32),
        compiler_params=_params(1),
        name="out_proj_ln",
    )(y2, x2, w, g, b)


def kernel(x, mem, w_in_ab, gmlp_ln_g, gmlp_ln_b, gmlp_w_s, gmlp_b_s, conv_w, conv_b, conv_gn_g, conv_gn_b, w_out_ab, w_qkv_c, w_out_c, mem_wq, mem_wk, mem_wv, mem_wo, ffn_w1, ffn_w3, ffn_w2, ln_g, ln_b):
    bn, s, d = x.shape
    n = bn * s
    row = lambda a: a.reshape(1, -1).astype(F32)
    bf = lambda a: a.astype(BF16)
    for layer in range(DEPTH):
        if layer % 2 == 0:
            e = layer // 2
            bs_full = jnp.repeat(gmlp_b_s[e].T, GROUP_DIM, axis=1)
            x = _even_mixer(x, bf(w_in_ab[e]), row(gmlp_ln_g[e]), row(gmlp_ln_b[e]), gmlp_w_s[e], bs_full,
                            conv_w[e], row(conv_b[e]), row(conv_gn_g[e]), row(conv_gn_b[e]), bf(w_out_ab[e]),
                            row(ln_g[layer, 0]), row(ln_b[layer, 0]))
        else:
            o = layer // 2
            x2 = x.reshape(n, d)
            q, k, v = _qkv(x2, bf(w_qkv_c[o]))
            y = _stick(q.reshape(bn, s, d), k.reshape(bn, s, d), v.reshape(bn, s, d))
            x = _proj_ln(y.reshape(n, d), x2, bf(w_out_c[o]), row(ln_g[layer, 0]), row(ln_b[layer, 0])).reshape(bn, s, d)
        kt, vm = _memkv(mem, bf(mem_wk[layer].T), bf(mem_wv[layer]))
        x = _cross(x, kt, vm, bf(mem_wq[layer]), bf(mem_wo[layer]), row(ln_g[layer, 1]), row(ln_b[layer, 1]))
        x = _ffn(x.reshape(n, d), bf(ffn_w1[layer]), bf(ffn_w3[layer]), bf(ffn_w2[layer]),
                 row(ln_g[layer, 2]), row(ln_b[layer, 2])).reshape(bn, s, d)
    return x
```
